```python
import math
import jax, jax.numpy as jnp
from jax import lax
import numpy as np

D_MODEL = 2048
BATCH = 16
SEQ = 256
DEPTH = 4
DEC_BATCH = 8
DEC_SEQ = 4096
PAST_LEN = 256

GRID_W = 64
ROPE_BASE = 10000.0
N_MIXERS = 2
N_FOURIER_LAYERS = (DEPTH + 1) // 2
N_MLA_LAYERS = DEPTH // 2
F_GROUPS = 4
F_GROUP_DIM = D_MODEL // F_GROUPS
N_HEADS = 16
Q_LORA_RANK = 512
KV_LORA_RANK = 512
NOPE_DIM = 128
ROPE_DIM = 64
V_HEAD_DIM = 128
QK_HEAD_DIM = NOPE_DIM + ROPE_DIM
AXIS_PAIRS = ROPE_DIM // 4
ATTN_SCALE = 1.0 / math.sqrt(QK_HEAD_DIM)
Q_BLOCK = 128
N_EXPERTS = 64
N_GROUPS = 8
EXPERTS_PER_GROUP = N_EXPERTS // N_GROUPS
TOP_K = 2
D_EXPERT = 512
ROUTED_SCALE = 2.5
MOE_BLOCK = 128
ALPHA = (2 * DEPTH) ** 0.25
BETA = (8 * DEPTH) ** -0.25
LN_EPS = 1e-5
RMS_EPS = 1e-6

kernel_name = "hybrid_fnet_mla_moe_diffusion_step"


def layer_norm(x, g=None, b=None):
    xf = x.astype(jnp.float32)
    mu = xf.mean(-1, keepdims=True)
    var = jnp.square(xf - mu).mean(-1, keepdims=True)
    y = (xf - mu) * lax.rsqrt(var + LN_EPS)
    if g is not None:
        y = y * g.astype(jnp.float32) + b.astype(jnp.float32)
    return y.astype(x.dtype)


def rms_norm(x, g):
    xf = x.astype(jnp.float32)
    y = xf * lax.rsqrt(jnp.square(xf).mean(-1, keepdims=True) + RMS_EPS) * g.astype(jnp.float32)
    return y.astype(x.dtype)


def modulate(x, shift, scale):
    return layer_norm(x) * (1 + scale) + shift


def axial_rope_angles(n):
    rows = n // GRID_W
    t_row = jnp.repeat(jnp.arange(rows, dtype=jnp.float32), GRID_W)
    t_col = jnp.tile(jnp.arange(GRID_W, dtype=jnp.float32), rows)
    inv = ROPE_BASE ** (-jnp.arange(AXIS_PAIRS, dtype=jnp.float32) / AXIS_PAIRS)
    ang = jnp.concatenate([t_row[:, None] * inv, t_col[:, None] * inv], -1)
    return jnp.cos(ang), jnp.sin(ang)


def apply_rope(x, cos, sin):
    xf = x.astype(jnp.float32).reshape(x.shape[:-1] + (ROPE_DIM // 2, 2))
    x0, x1 = xf[..., 0], xf[..., 1]
    out = jnp.stack([x0 * cos - x1 * sin, x0 * sin + x1 * cos], -1)
    return out.reshape(x.shape).astype(x.dtype)


def fourier_mix(u, w_in, w_out):
    B, L, D = u.shape
    h = (u @ w_in).reshape(B, L, F_GROUPS, F_GROUP_DIM).astype(jnp.float32)
    f = jnp.fft.fft2(h, axes=(1, 3), norm="ortho").real
    return f.astype(u.dtype).reshape(B, L, D) @ w_out


def attend(q, k, v):
    B, Lq, H, Dk = q.shape
    nb = Lq // Q_BLOCK
    qb = q.reshape(B, nb, Q_BLOCK, H, Dk).transpose(1, 0, 2, 3, 4)

    def one(qblk):
        s = jnp.einsum('bqhd,bkhd->bhqk', qblk, k, preferred_element_type=jnp.float32) * ATTN_SCALE
        p = jax.nn.softmax(s, axis=-1)
        return jnp.einsum('bhqk,bkhd->bqhd', p.astype(v.dtype), v)

    out = lax.map(one, qb)
    return out.transpose(1, 0, 2, 3, 4).reshape(B, Lq, H, v.shape[-1])


def mla_project(u, w_a, g_q, g_kv, w_uq):
    B, L, _ = u.shape
    a = u @ w_a
    cq = rms_norm(a[..., :Q_LORA_RANK], g_q)
    ckv = rms_norm(a[..., Q_LORA_RANK:Q_LORA_RANK + KV_LORA_RANK], g_kv)
    k_rope = a[..., Q_LORA_RANK + KV_LORA_RANK:]
    q = (cq @ w_uq).reshape(B, L, N_HEADS, QK_HEAD_DIM)
    return q, ckv, k_rope


def mla_expand(ckv, w_ukv):
    B, L, _ = ckv.shape
    kv = (ckv @ w_ukv).reshape(B, L, N_HEADS, NOPE_DIM + V_HEAD_DIM)
    return kv[..., :NOPE_DIM], kv[..., NOPE_DIM:]


def build_keys(k_nope, k_rope):
    kr = jnp.broadcast_to(k_rope[:, :, None, :], k_nope.shape[:3] + (ROPE_DIM,))
    return jnp.concatenate([k_nope, kr], -1)


def mla_context(u, w_a, g_q, g_kv, w_uq, w_ukv, w_o):
    B, L, _ = u.shape
    q, ckv, kr = mla_project(u, w_a, g_q, g_kv, w_uq)
    k_nope, v = mla_expand(ckv, w_ukv)
    o = attend(q, build_keys(k_nope, kr), v)
    return o.reshape(B, L, N_HEADS * V_HEAD_DIM) @ w_o, ckv, kr


def mla_latent(u, ckv_ctx, kr_ctx, w_a, g_q, g_kv, w_uq, w_ukv, w_o):
    B, L, _ = u.shape
    q, ckv, kr = mla_project(u, w_a, g_q, g_kv, w_uq)
    cos, sin = axial_rope_angles(L)
    q = jnp.concatenate([q[..., :NOPE_DIM], apply_rope(q[..., NOPE_DIM:], cos[:, None, :], sin[:, None, :])], -1)
    kr = apply_rope(kr, cos, sin)
    kn_l, v_l = mla_expand(ckv, w_ukv)
    kn_c, v_c = mla_expand(ckv_ctx.astype(u.dtype), w_ukv)
    k = build_keys(jnp.concatenate([kn_c, kn_l], 1), jnp.concatenate([kr_ctx.astype(u.dtype), kr], 1))
    v = jnp.concatenate([v_c, v_l], 1)
    o = attend(q, k, v)
    return o.reshape(B, L, N_HEADS * V_HEAD_DIM) @ w_o


def moe(u, w_router, b_router, w_in, w_out):
    B, L, D = u.shape
    x2 = u.reshape(-1, D)
    n = x2.shape[0]
    scores = jax.nn.sigmoid(x2.astype(jnp.float32) @ w_router.astype(jnp.float32))
    biased = scores + b_router.astype(jnp.float32)
    grouped = biased.reshape(n, N_GROUPS, EXPERTS_PER_GROUP)
    group_score = lax.top_k(grouped, 2)[0].sum(-1)
    g = jnp.argmax(group_score, -1)
    in_group = jnp.take_along_axis(grouped, g[:, None, None], axis=1)[:, 0]
    _, local = lax.top_k(in_group, TOP_K)
    expert = g[:, None] * EXPERTS_PER_GROUP + local
    gate = jnp.take_along_axis(scores, expert, -1)
    gate = ROUTED_SCALE * gate / gate.sum(-1, keepdims=True)

    flat_e = expert.reshape(-1)
    order = jnp.argsort(flat_e)
    sorted_e = flat_e[order]
    tok = order // TOP_K
    counts = jnp.bincount(flat_e, length=N_EXPERTS)
    padded = (counts + MOE_BLOCK - 1) // MOE_BLOCK * MOE_BLOCK
    start = jnp.cumsum(counts) - counts
    pend = jnp.cumsum(padded)
    pstart = pend - padded
    dest = pstart[sorted_e] + jnp.arange(n * TOP_K) - start[sorted_e]
    n_rows = -(-(n * TOP_K) // MOE_BLOCK) * MOE_BLOCK + N_EXPERTS * MOE_BLOCK
    n_blocks = n_rows // MOE_BLOCK
    xs = jnp.zeros((n_rows, D), u.dtype).at[dest].set(x2[tok])
    block_e = jnp.minimum(jnp.searchsorted(pend, jnp.arange(n_blocks) * MOE_BLOCK, side='right'), N_EXPERTS - 1)

    def run(args):
        xb, e = args
        h = xb @ w_in[e]
        return (jax.nn.silu(h[:, :D_EXPERT]) * h[:, D_EXPERT:]) @ w_out[e]

    ys = lax.map(run, (xs.reshape(n_blocks, MOE_BLOCK, D), block_e)).reshape(n_rows, D)
    contrib = ys[dest] * gate.reshape(-1)[order][:, None].astype(u.dtype)
    out = jnp.zeros((n, D), u.dtype).at[tok].add(contrib)
    return out.reshape(B, L, D)


def setup_inputs(seed: int = 0) -> dict:
    key = jax.random.key(seed)
    ks = jax.random.split(key, 32)
    D = D_MODEL
    nrm = jax.random.normal
    f32 = jnp.float32
    return {
        "x_prompt": nrm(ks[0], (BATCH, SEQ, D), f32),
        "x_sample": nrm(ks[1], (DEC_BATCH, DEC_SEQ, D), f32),
        "c": nrm(ks[2], (DEC_BATCH, D), f32),
        "cache_ckv": nrm(ks[3], (DEC_BATCH, N_MLA_LAYERS, PAST_LEN, KV_LORA_RANK), f32),
        "cache_krope": nrm(ks[4], (DEC_BATCH, N_MLA_LAYERS, PAST_LEN, ROPE_DIM), f32),
        "c_ctx": nrm(ks[5], (D,), f32),
        "w_ada": nrm(ks[6], (DEPTH, D, 6 * D), f32) * (0.5 * D ** -0.5),
        "b_ada": nrm(ks[7], (DEPTH, 6 * D), f32) * 0.01,
        "w_f_in": nrm(ks[8], (N_FOURIER_LAYERS, D, D), f32) * D ** -0.5,
        "w_f_out": nrm(ks[9], (N_FOURIER_LAYERS, D, D), f32) * (BETA * D ** -0.5),
        "w_mla_a": nrm(ks[10], (N_MLA_LAYERS, D, Q_LORA_RANK + KV_LORA_RANK + ROPE_DIM), f32) * D ** -0.5,
        "g_mla_q": 1.0 + 0.01 * nrm(ks[11], (N_MLA_LAYERS, Q_LORA_RANK), f32),
        "g_mla_kv": 1.0 + 0.01 * nrm(ks[12], (N_MLA_LAYERS, KV_LORA_RANK), f32),
        "w_mla_uq": nrm(ks[13], (N_MLA_LAYERS, Q_LORA_RANK, N_HEADS * QK_HEAD_DIM), f32) * Q_LORA_RANK ** -0.5,
        "w_mla_ukv": nrm(ks[14], (N_MLA_LAYERS, KV_LORA_RANK, N_HEADS * (NOPE_DIM + V_HEAD_DIM)), f32) * KV_LORA_RANK ** -0.5,
        "w_mla_o": nrm(ks[15], (N_MLA_LAYERS, N_HEADS * V_HEAD_DIM, D), f32) * (BETA * (N_HEADS * V_HEAD_DIM) ** -0.5),
        "ln_mix_g": 1.0 + 0.01 * nrm(ks[16], (DEPTH, D), f32),
        "ln_mix_b": 0.01 * nrm(ks[17], (DEPTH, D), f32),
        "ln_ffn_g": 1.0 + 0.01 * nrm(ks[18], (DEPTH, D), f32),
        "ln_ffn_b": 0.01 * nrm(ks[19], (DEPTH, D), f32),
        "w_router": nrm(ks[20], (D, N_EXPERTS), f32) * D ** -0.5,
        "b_router": 0.01 * nrm(ks[21], (N_EXPERTS,), f32),
        "w_exp_in": nrm(ks[22], (DEPTH, N_EXPERTS, D, 2 * D_EXPERT), f32) * D ** -0.5,
        "w_exp_out": nrm(ks[23], (DEPTH, N_EXPERTS, D_EXPERT, D), f32) * (BETA * D_EXPERT ** -0.5),
    }


def reference(x_prompt, x_sample, c, cache_ckv, cache_krope, c_ctx, w_ada, b_ada, w_f_in, w_f_out,
              w_mla_a, g_mla_q, g_mla_kv, w_mla_uq, w_mla_ukv, w_mla_o, ln_mix_g, ln_mix_b,
              ln_ffn_g, ln_ffn_b, w_router, b_router, w_exp_in, w_exp_out):
    y_p = x_prompt
    y_s = x_sample
    ckv_list, kr_list = [], []
    for l in range(DEPTH):
        j = l // N_MIXERS
        mod_p = (jax.nn.silu(c_ctx) @ w_ada[l] + b_ada[l])[None, None, :]
        mod_s = (jax.nn.silu(c) @ w_ada[l] + b_ada[l])[:, None, :]
        sh1p, sc1p, g1p, sh2p, sc2p, g2p = jnp.split(mod_p, 6, axis=-1)
        sh1s, sc1s, g1s, sh2s, sc2s, g2s = jnp.split(mod_s, 6, axis=-1)

        u_p = modulate(y_p, sh1p, sc1p)
        u_s = modulate(y_s, sh1s, sc1s)
        if l % N_MIXERS == 0:
            m_p = fourier_mix(u_p, w_f_in[j], w_f_out[j])
            m_s = fourier_mix(u_s, w_f_in[j], w_f_out[j])
        else:
            m_p, ckv_p, kr_p = mla_context(u_p, w_mla_a[j], g_mla_q[j], g_mla_kv[j], w_mla_uq[j],
                                           w_mla_ukv[j], w_mla_o[j])
            ckv_list.append(ckv_p)
            kr_list.append(kr_p)
            m_s = mla_latent(u_s, cache_ckv[:, j], cache_krope[:, j], w_mla_a[j], g_mla_q[j], g_mla_kv[j],
                             w_mla_uq[j], w_mla_ukv[j], w_mla_o[j])
        y_p = layer_norm(ALPHA * y_p + g1p * m_p, ln_mix_g[l], ln_mix_b[l])
        y_s = layer_norm(ALPHA * y_s + g1s * m_s, ln_mix_g[l], ln_mix_b[l])

        u_p = modulate(y_p, sh2p, sc2p)
        u_s = modulate(y_s, sh2s, sc2s)
        f_p = moe(u_p, w_router, b_router, w_exp_in[l], w_exp_out[l])
        f_s = moe(u_s, w_router, b_router, w_exp_in[l], w_exp_out[l])
        y_p = layer_norm(ALPHA * y_p + g2p * f_p, ln_ffn_g[l], ln_ffn_b[l])
        y_s = layer_norm(ALPHA * y_s + g2s * f_s, ln_ffn_g[l], ln_ffn_b[l])

    new_ckv = jnp.stack(ckv_list, axis=1)
    new_krope = jnp.stack(kr_list, axis=1)
    return (y_p, y_s, new_ckv, new_krope)
```

```python
import functools
import math
from typing import NamedTuple

import numpy as np
import jax
import jax.numpy as jnp
from jax import lax
from jax.experimental import pallas as pl
from jax.experimental.pallas import tpu as pltpu

f32 = jnp.float32
bf16 = jnp.bfloat16
i32 = jnp.int32

LN_EPS = 1e-5
RMS_EPS = 1e-6
ROPE_BASE = 10000.0
ROUTED_SCALE = 2.5
LOG2E = 1.4426950408889634
NEG_BIG = -1e30
MOD_ROWS = 16
VMEM_LIMIT_V7X = 56 * 1024 * 1024


class Cfg(NamedTuple):
    D: int = 2048
    DEPTH: int = 4
    BATCH: int = 16
    SEQ: int = 256
    DEC_BATCH: int = 8
    DEC_SEQ: int = 4096
    PAST: int = 256
    GRID_W: int = 64
    F_GROUPS: int = 4
    H: int = 16
    QL: int = 512
    KVL: int = 512
    NOPE: int = 128
    ROPE: int = 64
    VD: int = 128
    E: int = 64
    NG: int = 8
    DE: int = 512
    TM: int = 512
    TN_ADA: int = 1024
    FFT_J: int = 4
    FFT_KB: int = 4
    TKV: int = 256
    TQ: int = 1024
    CK: int = 256
    RC: int = 1024
    TE: int = 256

    @property
    def seg_len(self):
        return self.DEC_SEQ

    @property
    def n_seg(self):
        return 1 + self.DEC_BATCH

    @property
    def N(self):
        return self.n_seg * self.seg_len

    @property
    def R(self):
        return int(round(math.sqrt(self.seg_len)))

    @property
    def LK(self):
        return self.PAST + self.seg_len

    @property
    def alpha(self):
        return (2 * self.DEPTH) ** 0.25


def _cparams(sem):
    return pltpu.CompilerParams(dimension_semantics=sem, vmem_limit_bytes=VMEM_LIMIT_V7X)


def _ln(x):
    mu = jnp.mean(x, axis=-1, keepdims=True)
    xc = x - mu
    var = jnp.mean(xc * xc, axis=-1, keepdims=True)
    return xc * lax.rsqrt(var + LN_EPS)


def _sigmoid(x):
    return 1.0 / (1.0 + jnp.exp(-x))


def _dot(a, b):
    return jnp.dot(a, b, preferred_element_type=f32)


def _mod_spec(cfg, layer, which, tm):
    spt = cfg.seg_len // tm
    base = layer * MOD_ROWS * 6
    return pl.BlockSpec((1, 1, cfg.D), lambda i: (base + (i // spt) * 6 + which, 0, 0))


def _ada_call(cvec, w_ada, b_ada, cfg):
    D, n6, tn = cfg.D, 6 * cfg.D, cfg.TN_ADA

    def kern(c_ref, w_ref, b_ref, o_ref):
        c = c_ref[...]
        s = (c * _sigmoid(c)).astype(bf16)
        o_ref[0] = _dot(s, w_ref[0].astype(bf16)) + b_ref[0]

    return pl.pallas_call(
        kern,
        grid=(cfg.DEPTH, n6 // tn),
        in_specs=[pl.BlockSpec((MOD_ROWS, D), lambda l, j: (0, 0)),
                  pl.BlockSpec((1, D, tn), lambda l, j: (l, 0, j)),
                  pl.BlockSpec((1, 1, tn), lambda l, j: (l, 0, j))],
        out_specs=pl.BlockSpec((1, MOD_ROWS, tn), lambda l, j: (l, 0, j)),
        out_shape=jax.ShapeDtypeStruct((cfg.DEPTH, MOD_ROWS, n6), f32),
        compiler_params=_cparams(("arbitrary", "arbitrary")),
        name="ada",
    )(cvec, w_ada, b_ada.reshape(cfg.DEPTH, 1, n6))


def _fourier_in_call(y, modr, layer, w_in, cs, cfg):
    N, D, tm = cfg.N, cfg.D, cfg.TM
    gd = D // cfg.F_GROUPS

    def kern(y_ref, sh_ref, sc_ref, w_ref, cs_ref, a_ref):
        u = _ln(y_ref[...]) * (1.0 + sc_ref[0]) + sh_ref[0]
        h = _dot(u.astype(bf16), w_ref[...]).astype(bf16)
        for g in range(cfg.F_GROUPS):
            a = _dot(h[:, g * gd:(g + 1) * gd], cs_ref[...])
            a_ref[:, g * 2 * gd:(g + 1) * 2 * gd] = a.astype(bf16)

    return pl.pallas_call(
        kern,
        grid=(N // tm,),
        in_specs=[pl.BlockSpec((tm, D), lambda i: (i, 0)),
                  _mod_spec(cfg, layer, 0, tm), _mod_spec(cfg, layer, 1, tm),
                  pl.BlockSpec((D, D), lambda i: (0, 0)),
                  pl.BlockSpec((gd, 2 * gd), lambda i: (0, 0))],
        out_specs=pl.BlockSpec((tm, 2 * D), lambda i: (i, 0)),
        out_shape=jax.ShapeDtypeStruct((N, 2 * D), bf16),
        compiler_params=_cparams(("arbitrary",)),
        name="fourier_in",
    )(y, modr, modr, w_in, cs)


def _prompt_dft_call(a, c_seq, s_seq, cfg):
    D, L = cfg.D, cfg.SEQ
    gd = D // cfg.F_GROUPS
    a3 = a.reshape(cfg.N // L, L, 2 * D)

    def kern(c_ref, s_ref, x_ref, o_ref):
        x = x_ref[0]
        o_ref[0] = (_dot(c_ref[...], x[:, :gd]) + _dot(s_ref[...], x[:, gd:])).astype(bf16)

    out = pl.pallas_call(
        kern,
        grid=(cfg.BATCH, cfg.F_GROUPS),
        in_specs=[pl.BlockSpec((L, L), lambda s, g: (0, 0)),
                  pl.BlockSpec((L, L), lambda s, g: (0, 0)),
                  pl.BlockSpec((1, L, 2 * gd), lambda s, g: (s, 0, g))],
        out_specs=pl.BlockSpec((1, L, gd), lambda s, g: (s, 0, g)),
        out_shape=jax.ShapeDtypeStruct((cfg.BATCH, L, D), bf16),
        compiler_params=_cparams(("arbitrary", "arbitrary")),
        name="prompt_dft",
    )(c_seq, s_seq, a3)
    return out.reshape(cfg.seg_len, D)


def _fft_stage1_call(a, f2, tw_re, tw_im, cfg):
    D, R, J = cfg.D, cfg.R, cfg.FFT_J
    gd = D // cfg.F_GROUPS
    w = 2 * D
    a3 = a.reshape(cfg.n_seg, R, R * w)

    def kern(f_ref, twr_ref, twi_ref, x_ref, o_ref):
        for jj in range(J):
            twr = jnp.tile(twr_ref[jj], (1, gd // 128))
            twi = jnp.tile(twi_ref[jj], (1, gd // 128))
            for g in range(cfg.F_GROUPS):
                off = jj * w + g * 2 * gd
                pq = _dot(f_ref[...], x_ref[0, :, off:off + 2 * gd])
                p, q = pq[:R], pq[R:]
                t_re = p[:, :gd] - q[:, gd:]
                t_im = q[:, :gd] + p[:, gd:]
                o_ref[0, :, off:off + gd] = (t_re * twr - t_im * twi).astype(bf16)
                o_ref[0, :, off + gd:off + 2 * gd] = (t_re * twi + t_im * twr).astype(bf16)

    return pl.pallas_call(
        kern,
        grid=(cfg.DEC_BATCH, R // J),
        in_specs=[pl.BlockSpec((2 * R, R), lambda b, j: (0, 0)),
                  pl.BlockSpec((J, R, 128), lambda b, j: (j, 0, 0)),
                  pl.BlockSpec((J, R, 128), lambda b, j: (j, 0, 0)),
                  pl.BlockSpec((1, R, J * w), lambda b, j: (b + 1, 0, j))],
        out_specs=pl.BlockSpec((1, R, J * w), lambda b, j: (b, 0, j)),
        out_shape=jax.ShapeDtypeStruct((cfg.DEC_BATCH, R, R * w), bf16),
        compiler_params=_cparams(("arbitrary", "arbitrary")),
        name="fft_stage1",
    )(f2, tw_re, tw_im, a3)


def _fft_stage2_call(t, fc, fs, cfg):
    D, R, KB = cfg.D, cfg.R, cfg.FFT_KB
    gd = D // cfg.F_GROUPS
    w = 2 * D
    t3 = t.reshape(cfg.DEC_BATCH, R * R, w)

    def kern(fc_ref, fs_ref, x_ref, o_ref):
        for kk in range(KB):
            for g in range(cfg.F_GROUPS):
                x = x_ref[0, kk * R:(kk + 1) * R, g * 2 * gd:(g + 1) * 2 * gd]
                yv = _dot(fc_ref[...], x[:, :gd]) + _dot(fs_ref[...], x[:, gd:])
                o_ref[0, :, kk * D + g * gd:kk * D + (g + 1) * gd] = yv.astype(bf16)

    out = pl.pallas_call(
        kern,
        grid=(cfg.DEC_BATCH, R // KB),
        in_specs=[pl.BlockSpec((R, R), lambda b, k: (0, 0)),
                  pl.BlockSpec((R, R), lambda b, k: (0, 0)),
                  pl.BlockSpec((1, KB * R, w), lambda b, k: (b, k, 0))],
        out_specs=pl.BlockSpec((1, R, KB * D), lambda b, k: (b, 0, k)),
        out_shape=jax.ShapeDtypeStruct((cfg.DEC_BATCH, R, R * D), bf16),
        compiler_params=_cparams(("arbitrary", "arbitrary")),
        name="fft_stage2",
    )(fc, fs, t3)
    return out.reshape(cfg.DEC_BATCH * cfg.seg_len, D)


def _post_call(fs, w_out, y, modr, layer, which, ln_g, ln_b, cfg):
    N, D, tm = cfg.N, cfg.D, cfg.TM
    spt = cfg.seg_len // tm
    two = len(fs) == 2
    alpha = cfg.alpha

    def kern(*refs):
        if two:
            fp_ref, fs_ref, w_ref, y_ref, g_ref, lg_ref, lb_ref, o_ref = refs
            f = jnp.where(pl.program_id(0) < spt, fp_ref[...], fs_ref[...])
        else:
            f_ref, w_ref, y_ref, g_ref, lg_ref, lb_ref, o_ref = refs
            f = f_ref[...]
        m = _dot(f, w_ref[...])
        z = alpha * y_ref[...] + g_ref[0] * m
        o_ref[...] = _ln(z) * lg_ref[...] + lb_ref[...]

    if two:
        f_specs = [pl.BlockSpec((tm, D), lambda i: (jnp.minimum(i, spt - 1), 0)),
                   pl.BlockSpec((tm, D), lambda i: (jnp.maximum(i - spt, 0), 0))]
    else:
        f_specs = [pl.BlockSpec((tm, D), lambda i: (i, 0))]
    return pl.pallas_call(
        kern,
        grid=(N // tm,),
        in_specs=f_specs + [pl.BlockSpec((D, D), lambda i: (0, 0)),
                            pl.BlockSpec((tm, D), lambda i: (i, 0)),
                            _mod_spec(cfg, layer, which, tm),
                            pl.BlockSpec((1, D), lambda i: (0, 0)),
                            pl.BlockSpec((1, D), lambda i: (0, 0))],
        out_specs=pl.BlockSpec((tm, D), lambda i: (i, 0)),
        out_shape=jax.ShapeDtypeStruct((N, D), f32),
        compiler_params=_cparams(("arbitrary",)),
        name="mixer_post",
    )(*fs, w_out, y, modr, ln_g.reshape(1, D), ln_b.reshape(1, D))


def _mla_a_call(y, modr, layer, w_a2, g_q, g_kv, cos_t, sin_t, cfg):
    N, D, tm = cfg.N, cfg.D, cfg.TM
    QL, KVL, RP = cfg.QL, cfg.KVL, cfg.ROPE
    wa = w_a2.shape[1]
    o_kr = QL + KVL
    o_rot = o_kr + 128

    def rms(x, g):
        return x * lax.rsqrt(jnp.mean(x * x, axis=-1, keepdims=True) + RMS_EPS) * g

    def kern(y_ref, sh_ref, sc_ref, w_ref, gq_ref, gkv_ref, cos_ref, sin_ref,
             cq_ref, ckv_ref, kr_ref, krr_ref):
        u = _ln(y_ref[...]) * (1.0 + sc_ref[0]) + sh_ref[0]
        a = _dot(u.astype(bf16), w_ref[...])
        cq_ref[...] = rms(a[:, :QL], gq_ref[...]).astype(bf16)
        ckv_ref[...] = rms(a[:, QL:QL + KVL], gkv_ref[...])
        kr = a[:, o_kr:o_kr + RP]
        kr_ref[...] = kr
        krr_ref[...] = kr * cos_ref[...] + a[:, o_rot:o_rot + RP] * sin_ref[...]

    return pl.pallas_call(
        kern,
        grid=(N // tm,),
        in_specs=[pl.BlockSpec((tm, D), lambda i: (i, 0)),
                  _mod_spec(cfg, layer, 0, tm), _mod_spec(cfg, layer, 1, tm),
                  pl.BlockSpec((D, wa), lambda i: (0, 0)),
                  pl.BlockSpec((1, QL), lambda i: (0, 0)),
                  pl.BlockSpec((1, KVL), lambda i: (0, 0)),
                  pl.BlockSpec((tm, RP), lambda i: (i, 0)),
                  pl.BlockSpec((tm, RP), lambda i: (i, 0))],
        out_specs=[pl.BlockSpec((tm, QL), lambda i: (i, 0)),
                   pl.BlockSpec((tm, KVL), lambda i: (i, 0)),
                   pl.BlockSpec((tm, RP), lambda i: (i, 0)),
                   pl.BlockSpec((tm, RP), lambda i: (i, 0))],
        out_shape=[jax.ShapeDtypeStruct((N, QL), bf16),
                   jax.ShapeDtypeStruct((N, KVL), f32),
                   jax.ShapeDtypeStruct((N, RP), f32),
                   jax.ShapeDtypeStruct((N, RP), f32)],
        compiler_params=_cparams(("arbitrary",)),
        name="mla_down",
    )(y, modr, modr, w_a2, g_q.reshape(1, QL), g_kv.reshape(1, KVL), cos_t, sin_t)


def _q_call(cq, w_q, cos_t, sin_t, cfg):
    N, tm, H = cfg.N, cfg.TM, cfg.H
    QL, NP, RP = cfg.QL, cfg.NOPE, cfg.ROPE
    dk = NP + RP
    wq = NP + 2 * RP
    qscale = LOG2E / math.sqrt(dk)

    def kern(cq_ref, w_ref, cos_ref, sin_ref, q_ref):
        cq_t = cq_ref[...]
        cos, sin = cos_ref[...], sin_ref[...]
        for h in range(H):
            t = _dot(cq_t, w_ref[h])
            rope = t[:, NP:NP + RP] * cos + t[:, NP + RP:] * sin
            q_ref[h] = (jnp.concatenate([t[:, :NP], rope], axis=-1) * qscale).astype(bf16)

    return pl.pallas_call(
        kern,
        grid=(N // tm,),
        in_specs=[pl.BlockSpec((tm, QL), lambda i: (i, 0)),
                  pl.BlockSpec((H, QL, wq), lambda i: (0, 0, 0)),
                  pl.BlockSpec((tm, RP), lambda i: (i, 0)),
                  pl.BlockSpec((tm, RP), lambda i: (i, 0))],
        out_specs=pl.BlockSpec((H, tm, dk), lambda i: (0, i, 0)),
        out_shape=jax.ShapeDtypeStruct((H, N, dk), bf16),
        compiler_params=_cparams(("arbitrary",)),
        name="mla_q",
    )(cq, w_q, cos_t, sin_t)


def _kv_call(ckv_all, kr_all, w_kv, cfg):
    H, KVL, NP, RP, VD = cfg.H, cfg.KVL, cfg.NOPE, cfg.ROPE, cfg.VD
    LK, tkv, ns = cfg.LK, cfg.TKV, cfg.n_seg
    dk = NP + RP

    def kern(c_ref, kr_ref, w_ref, k_ref, v_ref):
        c = c_ref[0].astype(bf16)
        kr = kr_ref[0]
        for h in range(H):
            t = _dot(c, w_ref[h])
            k_ref[h, 0] = jnp.concatenate([t[:, :NP], kr], axis=-1).astype(bf16)
            v_ref[h, 0] = t[:, NP:].astype(bf16)

    return pl.pallas_call(
        kern,
        grid=(ns, LK // tkv),
        in_specs=[pl.BlockSpec((1, tkv, KVL), lambda s, i: (s, i, 0)),
                  pl.BlockSpec((1, tkv, RP), lambda s, i: (s, i, 0)),
                  pl.BlockSpec((H, KVL, NP + VD), lambda s, i: (0, 0, 0))],
        out_specs=[pl.BlockSpec((H, 1, tkv, dk), lambda s, i: (0, s, i, 0)),
                   pl.BlockSpec((H, 1, tkv, VD), lambda s, i: (0, s, i, 0))],
        out_shape=[jax.ShapeDtypeStruct((H, ns, LK, dk), bf16),
                   jax.ShapeDtypeStruct((H, ns, LK, VD), bf16)],
        compiler_params=_cparams(("arbitrary", "arbitrary")),
        name="mla_kv",
    )(ckv_all, kr_all, w_kv)


def _attention_call(q, k, v, cfg):
    H, NP, RP, VD = cfg.H, cfg.NOPE, cfg.ROPE, cfg.VD
    dk = NP + RP
    LK, TQ, CK, SEQ = cfg.LK, cfg.TQ, cfg.CK, cfg.SEQ
    nq = cfg.seg_len // TQ
    n_chunks = LK // CK
    sub = TQ // SEQ

    def scores(qb, kb):
        return lax.dot_general(qb, kb, (((1,), (1,)), ((), ())), preferred_element_type=f32)

    def kern(q_ref, k_ref, v_ref, o_ref):
        seg = pl.program_id(0)
        qi = pl.program_id(2)

        @pl.when(seg == 0)
        def _():
            for j in range(sub):
                qb = q_ref[0, j * SEQ:(j + 1) * SEQ, :]
                off = pl.multiple_of((qi * sub + j) * SEQ, SEQ)
                s = scores(qb, k_ref[0, 0, pl.ds(off, SEQ), :])
                p = jnp.exp2(s - jnp.max(s, axis=-1, keepdims=True))
                l = jnp.sum(p, axis=-1, keepdims=True)
                o = _dot(p.astype(bf16), v_ref[0, 0, pl.ds(off, SEQ), :])
                o_ref[j * SEQ:(j + 1) * SEQ, :] = (o / l).astype(bf16)

        @pl.when(seg != 0)
        def _():
            qb = q_ref[0]

            def body(c, carry):
                m, l, acc = carry
                off = pl.multiple_of(c * CK, CK)
                s = scores(qb, k_ref[0, 0, pl.ds(off, CK), :])
                m_new = jnp.maximum(m, jnp.max(s, axis=-1, keepdims=True))
                a = jnp.exp2(m - m_new)
                p = jnp.exp2(s - m_new)
                l = a * l + jnp.sum(p, axis=-1, keepdims=True)
                acc = a * acc + _dot(p.astype(bf16), v_ref[0, 0, pl.ds(off, CK), :])
                return m_new, l, acc

            init = (jnp.full((TQ, 1), NEG_BIG, f32), jnp.zeros((TQ, 1), f32), jnp.zeros((TQ, VD), f32))
            _, l, acc = lax.fori_loop(0, n_chunks, body, init)
            o_ref[...] = (acc / l).astype(bf16)

    return pl.pallas_call(
        kern,
        grid=(cfg.n_seg, H, nq),
        in_specs=[pl.BlockSpec((1, TQ, dk), lambda s, h, i: (h, s * nq + i, 0)),
                  pl.BlockSpec((1, 1, LK, dk), lambda s, h, i: (h, s, 0, 0)),
                  pl.BlockSpec((1, 1, LK, VD), lambda s, h, i: (h, s, 0, 0))],
        out_specs=pl.BlockSpec((TQ, VD), lambda s, h, i: (s * nq + i, h)),
        out_shape=jax.ShapeDtypeStruct((cfg.N, H * VD), bf16),
        compiler_params=_cparams(("arbitrary", "arbitrary", "arbitrary")),
        name="mla_attention",
    )(q, k, v)


def _router_call(y, modr, layer, wr2, b_router, cfg):
    N, D, tm, E, NG = cfg.N, cfg.D, cfg.TM, cfg.E, cfg.NG
    epg = E // NG

    def top2(x, jio):
        m1 = jnp.max(x, axis=0, keepdims=True)
        i1 = jnp.min(jnp.where(x == m1, jio, epg), axis=0, keepdims=True)
        x2 = jnp.where(jio == i1, -jnp.inf, x)
        m2 = jnp.max(x2, axis=0, keepdims=True)
        i2 = jnp.min(jnp.where(x2 == m2, jio, epg), axis=0, keepdims=True)
        return m1, i1, m2, i2

    def kern(y_ref, sh_ref, sc_ref, w_ref, b_ref, u_ref, e_ref, g_ref, cnt_ref):
        i = pl.program_id(0)
        u = _ln(y_ref[...]) * (1.0 + sc_ref[0]) + sh_ref[0]
        u_ref[...] = u
        u_hi = u.astype(bf16)
        u_lo = (u - u_hi.astype(f32)).astype(bf16)
        out = _dot(jnp.concatenate([u_hi, u_lo], axis=1), w_ref[...])
        lt = out.T
        logits = lt[:E] + lt[E:]
        scores = _sigmoid(logits)
        biased = scores + b_ref[...]
        jio = lax.broadcasted_iota(i32, (epg, tm), 0)
        best = gi = None
        for g in range(NG):
            m1, _, m2, _ = top2(biased[g * epg:(g + 1) * epg], jio)
            gs = m1 + m2
            if g == 0:
                best, gi = gs, jnp.zeros((1, tm), i32)
            else:
                better = gs > best
                best = jnp.where(better, gs, best)
                gi = jnp.where(better, g, gi)
        sel_b = biased[:epg]
        sel_s = scores[:epg]
        for g in range(1, NG):
            pick = gi == g
            sel_b = jnp.where(pick, biased[g * epg:(g + 1) * epg], sel_b)
            sel_s = jnp.where(pick, scores[g * epg:(g + 1) * epg], sel_s)
        _, i1, _, i2 = top2(sel_b, jio)
        s1 = jnp.sum(jnp.where(jio == i1, sel_s, 0.0), axis=0, keepdims=True)
        s2 = jnp.sum(jnp.where(jio == i2, sel_s, 0.0), axis=0, keepdims=True)
        den = s1 + s2
        e1 = gi * epg + i1
        e2 = gi * epg + i2
        e_ref[0:1, :] = e1
        e_ref[1:2, :] = e2
        rio = lax.broadcasted_iota(i32, (128, tm), 0)
        gates = jnp.where(rio == 0, ROUTED_SCALE * s1 / den,
                          jnp.where(rio == 1, ROUTED_SCALE * s2 / den, 0.0))
        g_ref[...] = gates.T
        eio = lax.broadcasted_iota(i32, (E, tm), 0)
        hits = jnp.where(eio == e1, 1.0, 0.0) + jnp.where(eio == e2, 1.0, 0.0)
        part = jnp.sum(hits, axis=1, keepdims=True)

        @pl.when(i == 0)
        def _():
            cnt_ref[...] = jnp.zeros_like(cnt_ref)

        cnt_ref[...] += part

    return pl.pallas_call(
        kern,
        grid=(N // tm,),
        in_specs=[pl.BlockSpec((tm, D), lambda i: (i, 0)),
                  _mod_spec(cfg, layer, 3, tm), _mod_spec(cfg, layer, 4, tm),
                  pl.BlockSpec((2 * D, 2 * E), lambda i: (0, 0)),
                  pl.BlockSpec((E, 1), lambda i: (0, 0))],
        out_specs=[pl.BlockSpec((tm, D), lambda i: (i, 0)),
                   pl.BlockSpec((2, tm), lambda i: (0, i)),
                   pl.BlockSpec((tm, 128), lambda i: (i, 0)),
                   pl.BlockSpec((E, 128), lambda i: (0, 0))],
        out_shape=[jax.ShapeDtypeStruct((N, D), f32),
                   jax.ShapeDtypeStruct((2, N), i32),
                   jax.ShapeDtypeStruct((N, 128), f32),
                   jax.ShapeDtypeStruct((E, 128), f32)],
        compiler_params=_cparams(("arbitrary",)),
        name="moe_router",
    )(y, modr, modr, wr2, b_router.reshape(E, 1))


def _rank_call(e_flat, pstart, tri, cfg):
    E, C = cfg.E, cfg.RC
    n_asg = e_flat.shape[1]

    def kern(e_ref, ps_ref, tri_ref, d_ref, base_ref):
        @pl.when(pl.program_id(0) == 0)
        def _():
            base_ref[...] = jnp.zeros_like(base_ref)

        eio = lax.broadcasted_iota(i32, (E, C), 0)
        oh = eio == e_ref[...]
        incl = _dot(jnp.where(oh, 1.0, 0.0).astype(bf16), tri_ref[...])
        base = base_ref[:, 0:1]
        pos = ps_ref[...] + base + incl - 1.0
        d_ref[...] = jnp.sum(jnp.where(oh, pos, 0.0), axis=0, keepdims=True).astype(i32)
        base_ref[...] += incl[:, C - 1:C]

    return pl.pallas_call(
        kern,
        grid=(n_asg // C,),
        in_specs=[pl.BlockSpec((1, C), lambda i: (0, i)),
                  pl.BlockSpec((E, 1), lambda i: (0, 0)),
                  pl.BlockSpec((C, C), lambda i: (0, 0))],
        out_specs=pl.BlockSpec((1, C), lambda i: (0, i)),
        out_shape=jax.ShapeDtypeStruct((1, n_asg), i32),
        scratch_shapes=[pltpu.VMEM((E, 128), f32)],
        compiler_params=_cparams(("arbitrary",)),
        name="moe_rank",
    )(e_flat, pstart, tri)


def _dispatch_call(dest3, u, xs0, cfg):
    N, D, tm = cfg.N, cfg.D, cfg.TM

    def kern(d_ref, u_ref, xs_in_ref, xs_ref, sem):
        del xs_in_ref

        def issue(r, c):
            for k in range(2):
                pltpu.make_async_copy(u_ref.at[pl.ds(r, 1), :],
                                      xs_ref.at[pl.ds(d_ref[0, k, r], 1), :], sem).start()
            return c

        lax.fori_loop(0, tm, issue, 0)
        for k in range(2):
            pltpu.make_async_copy(u_ref, xs_ref.at[pl.ds(0, tm), :], sem).wait()

    return pl.pallas_call(
        kern,
        grid=(N // tm,),
        in_specs=[pl.BlockSpec((1, 2, tm), lambda i: (i, 0, 0), memory_space=pltpu.SMEM),
                  pl.BlockSpec((tm, D), lambda i: (i, 0)),
                  pl.BlockSpec(memory_space=pl.ANY)],
        out_specs=pl.BlockSpec(memory_space=pl.ANY),
        out_shape=jax.ShapeDtypeStruct(xs0.shape, xs0.dtype),
        scratch_shapes=[pltpu.SemaphoreType.DMA(())],
        input_output_aliases={2: 0},
        compiler_params=_cparams(("arbitrary",)),
        name="moe_dispatch",
    )(dest3, u, xs0)


def _expert_call(tile_e, tile_src, tile_used, xs, w_exp_in, w_exp_out, layer, cfg):
    D, DE, TE = cfg.D, cfg.DE, cfg.TE
    n_rows = xs.shape[0]
    n_tiles = n_rows // TE

    def kern(te_ref, ts_ref, tu_ref, x_ref, wi_ref, wo_ref, y_ref, wi_bf, wo_bf):
        i = pl.program_id(0)
        changed = te_ref[i] != te_ref[jnp.maximum(i - 1, 0)]

        @pl.when((i == 0) | changed)
        def _():
            wi_bf[...] = wi_ref[0, 0].astype(bf16)
            wo_bf[...] = wo_ref[0, 0].astype(bf16)

        @pl.when(tu_ref[i] == 1)
        def _():
            h = _dot(x_ref[...].astype(bf16), wi_bf[...])
            act = h[:, :DE] * _sigmoid(h[:, :DE]) * h[:, DE:]
            y_ref[...] = _dot(act.astype(bf16), wo_bf[...])

        @pl.when(tu_ref[i] == 0)
        def _():
            y_ref[...] = jnp.zeros_like(y_ref)

    grid_spec = pltpu.PrefetchScalarGridSpec(
        num_scalar_prefetch=3,
        grid=(n_tiles,),
        in_specs=[pl.BlockSpec((TE, D), lambda i, te, ts, tu: (ts[i], 0)),
                  pl.BlockSpec((1, 1, D, 2 * DE), lambda i, te, ts, tu: (layer, te[i], 0, 0)),
                  pl.BlockSpec((1, 1, DE, D), lambda i, te, ts, tu: (layer, te[i], 0, 0))],
        out_specs=pl.BlockSpec((TE, D), lambda i, te, ts, tu: (i, 0)),
        scratch_shapes=[pltpu.VMEM((D, 2 * DE), bf16), pltpu.VMEM((DE, D), bf16)],
    )
    return pl.pallas_call(
        kern,
        grid_spec=grid_spec,
        out_shape=jax.ShapeDtypeStruct((n_rows, D), f32),
        compiler_params=_cparams(("arbitrary",)),
        name="moe_experts",
    )(tile_e, tile_src, tile_used, xs, w_exp_in, w_exp_out)


def _combine_call(dest3, y, gt, ys, modr, layer, ln_g, ln_b, cfg):
    N, D, tm = cfg.N, cfg.D, cfg.TM
    alpha = cfg.alpha

    def kern(d_ref, y_ref, gt_ref, g2_ref, lg_ref, lb_ref, ys_ref, o_ref, r0, r1, sem):
        bufs = (r0, r1)

        def issue(r, c):
            for k in range(2):
                pltpu.make_async_copy(ys_ref.at[pl.ds(d_ref[0, k, r], 1), :],
                                      bufs[k].at[pl.ds(r, 1), :], sem).start()
            return c

        lax.fori_loop(0, tm, issue, 0)
        for k in range(2):
            pltpu.make_async_copy(ys_ref.at[pl.ds(0, tm), :], bufs[k], sem).wait()
        gt_t = gt_ref[...]
        f = gt_t[:, 0:1] * r0[...] + gt_t[:, 1:2] * r1[...]
        z = alpha * y_ref[...] + g2_ref[0] * f
        o_ref[...] = _ln(z) * lg_ref[...] + lb_ref[...]

    return pl.pallas_call(
        kern,
        grid=(N // tm,),
        in_specs=[pl.BlockSpec((1, 2, tm), lambda i: (i, 0, 0), memory_space=pltpu.SMEM),
                  pl.BlockSpec((tm, D), lambda i: (i, 0)),
                  pl.BlockSpec((tm, 128), lambda i: (i, 0)),
                  _mod_spec(cfg, layer, 5, tm),
                  pl.BlockSpec((1, D), lambda i: (0, 0)),
                  pl.BlockSpec((1, D), lambda i: (0, 0)),
                  pl.BlockSpec(memory_space=pl.ANY)],
        out_specs=pl.BlockSpec((tm, D), lambda i: (i, 0)),
        out_shape=jax.ShapeDtypeStruct((N, D), f32),
        scratch_shapes=[pltpu.VMEM((tm, D), f32), pltpu.VMEM((tm, D), f32),
                        pltpu.SemaphoreType.DMA(())],
        compiler_params=_cparams(("arbitrary",)),
        name="moe_combine",
    )(dest3, y, gt, modr, ln_g.reshape(1, D), ln_b.reshape(1, D), ys)


def _moe_layer(y, modr, layer, wr2, b_router, tri, w_exp_in, w_exp_out, ln_g, ln_b, cfg):
    N, D, E, TE, tm = cfg.N, cfg.D, cfg.E, cfg.TE, cfg.TM
    u, e2, gt, cnt = _router_call(y, modr, layer, wr2, b_router, cfg)
    counts = cnt[:, 0].astype(i32)
    padded = (counts + TE - 1) // TE * TE
    pend = jnp.cumsum(padded)
    pstart = pend - padded
    n_rows = 2 * N + E * TE
    n_tiles = n_rows // TE
    tile_start = jnp.arange(n_tiles, dtype=i32) * TE
    n_used = pend[-1] // TE
    tile_src = jnp.minimum(jnp.arange(n_tiles, dtype=i32), n_used - 1)
    tile_e = jnp.minimum(jnp.searchsorted(pend, tile_src * TE, side="right"), E - 1).astype(i32)
    tile_used = (tile_start < pend[-1]).astype(i32)

    dest = _rank_call(e2.reshape(1, 2 * N), pstart.astype(f32).reshape(E, 1), tri, cfg)
    dest3 = dest.reshape(2, N // tm, tm).transpose(1, 0, 2)
    xs = _dispatch_call(dest3, u, jnp.zeros((n_rows, D), f32), cfg)
    ys = _expert_call(tile_e, tile_src, tile_used, xs, w_exp_in, w_exp_out, layer, cfg)
    return _combine_call(dest3, y, gt, ys, modr, layer, ln_g, ln_b, cfg)


def _dft_tables(cfg):
    D, R, L = cfg.D, cfg.R, cfg.SEQ
    gd = D // cfg.F_GROUPS

    def cs(n, scale):
        k = np.arange(n)
        ang = 2.0 * np.pi * ((k[:, None] * k[None, :]) % n) / n
        return np.cos(ang) * scale, np.sin(ang) * scale

    cc, sc = cs(gd, 1.0 / math.sqrt(gd))
    chan = jnp.asarray(np.concatenate([cc, -sc], axis=1), bf16)
    cl, sl = cs(L, 1.0 / math.sqrt(L))
    cr, sr = cs(R, 1.0 / math.sqrt(R))
    f2 = jnp.asarray(np.concatenate([cr, -sr], axis=0), bf16)
    k1 = np.arange(R)
    ang = 2.0 * np.pi * (k1[None, :] * k1[:, None]) / (R * R)
    tw_re = jnp.asarray(np.repeat(np.cos(ang)[:, :, None], 128, axis=2), f32)
    tw_im = jnp.asarray(np.repeat(-np.sin(ang)[:, :, None], 128, axis=2), f32)
    return dict(chan=chan, c_seq=jnp.asarray(cl, bf16), s_seq=jnp.asarray(sl, bf16), f2=f2,
                fc=jnp.asarray(cr, bf16), fs=jnp.asarray(sr, bf16), tw_re=tw_re, tw_im=tw_im)


def _rot_cols(w):
    shp = w.shape
    w2 = w.reshape(shp[:-1] + (shp[-1] // 2, 2))
    return jnp.stack([-w2[..., 1], w2[..., 0]], axis=-1).reshape(shp)


def _rope_tables(cfg):
    L, RP = cfg.seg_len, cfg.ROPE
    pairs = RP // 4
    rows = L // cfg.GRID_W
    t_row = jnp.repeat(jnp.arange(rows, dtype=f32), cfg.GRID_W)
    t_col = jnp.tile(jnp.arange(cfg.GRID_W, dtype=f32), rows)
    inv = ROPE_BASE ** (-jnp.arange(pairs, dtype=f32) / pairs)
    ang = jnp.concatenate([t_row[:, None] * inv, t_col[:, None] * inv], -1)
    cos = jnp.repeat(jnp.cos(ang), 2, axis=-1)
    sin = jnp.repeat(jnp.sin(ang), 2, axis=-1)
    cos_t = jnp.concatenate([jnp.ones((L, RP), f32), jnp.tile(cos, (cfg.DEC_BATCH, 1))], 0)
    sin_t = jnp.concatenate([jnp.zeros((L, RP), f32), jnp.tile(sin, (cfg.DEC_BATCH, 1))], 0)
    return cos_t, sin_t


def _mla_weights(w_a, w_uq, w_ukv, cfg):
    D, H, QL, KVL, NP, RP, VD = cfg.D, cfg.H, cfg.QL, cfg.KVL, cfg.NOPE, cfg.ROPE, cfg.VD
    kr_w = w_a[:, QL + KVL:]
    z = jnp.zeros((D, 128 - RP), f32)
    w_a2 = jnp.concatenate([w_a[:, :QL + KVL], kr_w, z, _rot_cols(kr_w), z], axis=1).astype(bf16)
    wq = w_uq.reshape(QL, H, NP + RP)
    w_q = jnp.concatenate([wq, _rot_cols(wq[..., NP:])], axis=-1).transpose(1, 0, 2).astype(bf16)
    w_kv = w_ukv.reshape(KVL, H, NP + VD).transpose(1, 0, 2).astype(bf16)
    return w_a2, w_q, w_kv


def _forward(cfg, x_prompt, x_sample, c, cache_ckv, cache_krope, c_ctx, w_ada, b_ada, w_f_in, w_f_out,
             w_mla_a, g_mla_q, g_mla_kv, w_mla_uq, w_mla_ukv, w_mla_o, ln_mix_g, ln_mix_b,
             ln_ffn_g, ln_ffn_b, w_router, b_router, w_exp_in, w_exp_out):
    assert cfg.BATCH * cfg.SEQ == cfg.seg_len and cfg.R * cfg.R == cfg.seg_len
    assert cfg.TQ % cfg.SEQ == 0 and cfg.LK % cfg.CK == 0 and cfg.LK % cfg.TKV == 0
    N, D, L, E = cfg.N, cfg.D, cfg.seg_len, cfg.E
    y = jnp.concatenate([x_prompt.reshape(L, D), x_sample.reshape(cfg.DEC_BATCH * L, D)], axis=0)

    cvec = jnp.concatenate([c_ctx[None], c, jnp.zeros((MOD_ROWS - cfg.n_seg, D), f32)], axis=0)
    mod = _ada_call(cvec, w_ada, b_ada, cfg)
    modr = mod.reshape(cfg.DEPTH * MOD_ROWS * 6, 1, D)

    tabs = _dft_tables(cfg)
    cos_t, sin_t = _rope_tables(cfg)
    w_hi = w_router.astype(bf16)
    w_lo = (w_router - w_hi.astype(f32)).astype(bf16)
    wr2 = jnp.concatenate([jnp.concatenate([w_hi, w_lo], 1),
                           jnp.concatenate([w_hi, jnp.zeros_like(w_lo)], 1)], 0)
    rc = cfg.RC
    tri = jnp.asarray(np.triu(np.ones((rc, rc), np.float32)), bf16)

    ckv_out, kr_out = [], []
    for l in range(cfg.DEPTH):
        j = l // 2
        if l % 2 == 0:
            a = _fourier_in_call(y, modr, l, w_f_in[j].astype(bf16), tabs["chan"], cfg)
            f_p = _prompt_dft_call(a, tabs["c_seq"], tabs["s_seq"], cfg)
            t = _fft_stage1_call(a, tabs["f2"], tabs["tw_re"], tabs["tw_im"], cfg)
            f_s = _fft_stage2_call(t, tabs["fc"], tabs["fs"], cfg)
            y = _post_call((f_p, f_s), w_f_out[j].astype(bf16), y, modr, l, 2, ln_mix_g[l], ln_mix_b[l], cfg)
        else:
            w_a2, w_q, w_kv = _mla_weights(w_mla_a[j], w_mla_uq[j], w_mla_ukv[j], cfg)
            cq, ckv, kr, krr = _mla_a_call(y, modr, l, w_a2, g_mla_q[j], g_mla_kv[j], cos_t, sin_t, cfg)
            ckv_out.append(ckv[:L].reshape(cfg.BATCH, cfg.SEQ, cfg.KVL))
            kr_out.append(kr[:L].reshape(cfg.BATCH, cfg.SEQ, cfg.ROPE))
            q = _q_call(cq, w_q, cos_t, sin_t, cfg)
            pad = cfg.PAST
            ckv_all = jnp.concatenate([
                jnp.concatenate([ckv[:L], jnp.zeros((pad, cfg.KVL), f32)], 0)[None],
                jnp.concatenate([cache_ckv[:, j], ckv[L:].reshape(cfg.DEC_BATCH, L, cfg.KVL)], 1)], 0)
            kr_all = jnp.concatenate([
                jnp.concatenate([krr[:L], jnp.zeros((pad, cfg.ROPE), f32)], 0)[None],
                jnp.concatenate([cache_krope[:, j], krr[L:].reshape(cfg.DEC_BATCH, L, cfg.ROPE)], 1)], 0)
            k, v = _kv_call(ckv_all, kr_all, w_kv, cfg)
            o = _attention_call(q, k, v, cfg)
            y = _post_call((o,), w_mla_o[j].astype(bf16), y, modr, l, 2, ln_mix_g[l], ln_mix_b[l], cfg)
        y = _moe_layer(y, modr, l, wr2, b_router, tri, w_exp_in, w_exp_out, ln_ffn_g[l], ln_ffn_b[l], cfg)

    y_p = y[:L].reshape(cfg.BATCH, cfg.SEQ, D)
    y_s = y[L:].reshape(cfg.DEC_BATCH, L, D)
    return y_p, y_s, jnp.stack(ckv_out, axis=1), jnp.stack(kr_out, axis=1)


def kernel(x_prompt, x_sample, c, cache_ckv, cache_krope, c_ctx, w_ada, b_ada, w_f_in, w_f_out, w_mla_a, g_mla_q, g_mla_kv, w_mla_uq, w_mla_ukv, w_mla_o, ln_mix_g, ln_mix_b, ln_ffn_g, ln_ffn_b, w_router, b_router, w_exp_in, w_exp_out):
    return _forward(Cfg(), x_prompt, x_sample, c, cache_ckv, cache_krope, c_ctx, w_ada, b_ada, w_f_in,
                    w_f_out, w_mla_a, g_mla_q, g_mla_kv, w_mla_uq, w_mla_ukv, w_mla_o, ln_mix_g,
                    ln_mix_b, ln_ffn_g, ln_ffn_b, w_router, b_router, w_exp_in, w_exp_out)
```

```python
import functools
import math
from typing import NamedTuple

import numpy as np
import jax
import jax.numpy as jnp
from jax import lax
from jax.experimental import pallas as pl
from jax.experimental.pallas import tpu as pltpu

f32 = jnp.float32
bf16 = jnp.bfloat16
i32 = jnp.int32

LN_EPS = 1e-5
RMS_EPS = 1e-6
ROPE_BASE = 10000.0
ROUTED_SCALE = 2.5
LOG2E = 1.4426950408889634
NEG_BIG = -1e30
MOD_ROWS = 16
VMEM_LIMIT_V7X = 56 * 1024 * 1024


class Cfg(NamedTuple):
    D: int = 2048
    DEPTH: int = 4
    BATCH: int = 16
    SEQ: int = 256
    DEC_BATCH: int = 8
    DEC_SEQ: int = 4096
    PAST: int = 256
    GRID_W: int = 64
    F_GROUPS: int = 4
    H: int = 16
    QL: int = 512
    KVL: int = 512
    NOPE: int = 128
    ROPE: int = 64
    VD: int = 128
    E: int = 64
    NG: int = 8
    DE: int = 512
    TM: int = 512
    TN_ADA: int = 1024
    FFT_J: int = 16
    FFT_KB: int = 16
    TKV: int = 256
    TQ: int = 512
    HB: int = 2
    RC: int = 1024
    TE: int = 256

    @property
    def seg_len(self):
        return self.DEC_SEQ

    @property
    def n_seg(self):
        return 1 + self.DEC_BATCH

    @property
    def N(self):
        return self.n_seg * self.seg_len

    @property
    def R(self):
        return int(round(math.sqrt(self.seg_len)))

    @property
    def LK(self):
        return self.PAST + self.seg_len

    @property
    def alpha(self):
        return (2 * self.DEPTH) ** 0.25


def _cparams(sem):
    return pltpu.CompilerParams(dimension_semantics=sem, vmem_limit_bytes=VMEM_LIMIT_V7X)


def _ln(x):
    mu = jnp.mean(x, axis=-1, keepdims=True)
    xc = x - mu
    var = jnp.mean(xc * xc, axis=-1, keepdims=True)
    return xc * lax.rsqrt(var + LN_EPS)


def _sigmoid(x):
    return 1.0 / (1.0 + jnp.exp(-x))


def _dot(a, b):
    return jnp.dot(a, b, preferred_element_type=f32)


def _mod_spec(cfg, layer, which, tm):
    spt = cfg.seg_len // tm
    base = layer * MOD_ROWS * 6
    return pl.BlockSpec((1, 1, cfg.D), lambda i: (base + (i // spt) * 6 + which, 0, 0))


def _ada_call(cvec, w_ada, b_ada, cfg):
    D, n6, tn = cfg.D, 6 * cfg.D, cfg.TN_ADA

    def kern(c_ref, w_ref, b_ref, o_ref):
        c = c_ref[...]
        s = (c * _sigmoid(c)).astype(bf16)
        o_ref[0] = _dot(s, w_ref[0].astype(bf16)) + b_ref[0]

    return pl.pallas_call(
        kern,
        grid=(cfg.DEPTH, n6 // tn),
        in_specs=[pl.BlockSpec((MOD_ROWS, D), lambda l, j: (0, 0)),
                  pl.BlockSpec((1, D, tn), lambda l, j: (l, 0, j)),
                  pl.BlockSpec((1, 1, tn), lambda l, j: (l, 0, j))],
        out_specs=pl.BlockSpec((1, MOD_ROWS, tn), lambda l, j: (l, 0, j)),
        out_shape=jax.ShapeDtypeStruct((cfg.DEPTH, MOD_ROWS, n6), f32),
        compiler_params=_cparams(("arbitrary", "arbitrary")),
        name="ada",
    )(cvec, w_ada, b_ada.reshape(cfg.DEPTH, 1, n6))


def _fourier_in_call(y, modr, layer, w_in, cs, cfg):
    N, D, tm = cfg.N, cfg.D, cfg.TM
    gd = D // cfg.F_GROUPS

    def kern(y_ref, sh_ref, sc_ref, w_ref, cs_ref, a_ref):
        u = _ln(y_ref[...]) * (1.0 + sc_ref[0]) + sh_ref[0]
        h = _dot(u.astype(bf16), w_ref[...]).astype(bf16)
        for g in range(cfg.F_GROUPS):
            a = _dot(h[:, g * gd:(g + 1) * gd], cs_ref[...])
            a_ref[:, g * 2 * gd:(g + 1) * 2 * gd] = a.astype(bf16)

    return pl.pallas_call(
        kern,
        grid=(N // tm,),
        in_specs=[pl.BlockSpec((tm, D), lambda i: (i, 0)),
                  _mod_spec(cfg, layer, 0, tm), _mod_spec(cfg, layer, 1, tm),
                  pl.BlockSpec((D, D), lambda i: (0, 0)),
                  pl.BlockSpec((gd, 2 * gd), lambda i: (0, 0))],
        out_specs=pl.BlockSpec((tm, 2 * D), lambda i: (i, 0)),
        out_shape=jax.ShapeDtypeStruct((N, 2 * D), bf16),
        compiler_params=_cparams(("arbitrary",)),
        name="fourier_in",
    )(y, modr, modr, w_in, cs)


def _prompt_dft_call(a, c_seq, s_seq, cfg):
    D, L = cfg.D, cfg.SEQ
    gd = D // cfg.F_GROUPS
    a3 = a.reshape(cfg.N // L, L, 2 * D)

    def kern(c_ref, s_ref, x_ref, o_ref):
        x = x_ref[0]
        o_ref[0] = (_dot(c_ref[...], x[:, :gd]) + _dot(s_ref[...], x[:, gd:])).astype(bf16)

    out = pl.pallas_call(
        kern,
        grid=(cfg.BATCH, cfg.F_GROUPS),
        in_specs=[pl.BlockSpec((L, L), lambda s, g: (0, 0)),
                  pl.BlockSpec((L, L), lambda s, g: (0, 0)),
                  pl.BlockSpec((1, L, 2 * gd), lambda s, g: (s, 0, g))],
        out_specs=pl.BlockSpec((1, L, gd), lambda s, g: (s, 0, g)),
        out_shape=jax.ShapeDtypeStruct((cfg.BATCH, L, D), bf16),
        compiler_params=_cparams(("arbitrary", "arbitrary")),
        name="prompt_dft",
    )(c_seq, s_seq, a3)
    return out.reshape(cfg.seg_len, D)


def _fft_stage1_call(a, k1m, tw_re, tw_im, cfg):
    D, R, J = cfg.D, cfg.R, cfg.FFT_J
    gd = D // cfg.F_GROUPS
    w = 2 * D
    rj = R * J
    a4 = a.reshape(cfg.n_seg, R, R, w)

    def kern(km_ref, twr_ref, twi_ref, x_ref, o_ref):
        pq = _dot(km_ref[...], x_ref[0].reshape(rj, 2 * gd))
        p, q = pq[:rj], pq[rj:]
        t_re = p[:, :gd] - q[:, gd:]
        t_im = q[:, :gd] + p[:, gd:]
        twr = jnp.tile(twr_ref[0], (1, gd // 128))
        twi = jnp.tile(twi_ref[0], (1, gd // 128))
        o_ref[0, :, :, :gd] = (t_re * twr - t_im * twi).astype(bf16).reshape(R, J, gd)
        o_ref[0, :, :, gd:] = (t_re * twi + t_im * twr).astype(bf16).reshape(R, J, gd)

    return pl.pallas_call(
        kern,
        grid=(cfg.DEC_BATCH, R // J, cfg.F_GROUPS),
        in_specs=[pl.BlockSpec((2 * rj, rj), lambda b, j, g: (0, 0)),
                  pl.BlockSpec((1, rj, 128), lambda b, j, g: (j, 0, 0)),
                  pl.BlockSpec((1, rj, 128), lambda b, j, g: (j, 0, 0)),
                  pl.BlockSpec((1, R, J, 2 * gd), lambda b, j, g: (b + 1, 0, j, g))],
        out_specs=pl.BlockSpec((1, R, J, 2 * gd), lambda b, j, g: (b, 0, j, g)),
        out_shape=jax.ShapeDtypeStruct((cfg.DEC_BATCH, R, R, w), bf16),
        compiler_params=_cparams(("arbitrary", "arbitrary", "arbitrary")),
        name="fft_stage1",
    )(k1m, tw_re, tw_im, a4)


def _fft_stage2_call(t4, k2c, k2s, cfg):
    D, R, KB = cfg.D, cfg.R, cfg.FFT_KB
    gd = D // cfg.F_GROUPS
    w = 2 * D
    t3 = t4.reshape(cfg.DEC_BATCH, R * R, w)

    def kern(kc_ref, ks_ref, x_ref, o_ref):
        x = x_ref[0]
        yv = _dot(kc_ref[...], x[:, :gd]) + _dot(ks_ref[...], x[:, gd:])
        o_ref[0] = yv.astype(bf16).reshape(R, KB, gd)

    out = pl.pallas_call(
        kern,
        grid=(cfg.DEC_BATCH, R // KB, cfg.F_GROUPS),
        in_specs=[pl.BlockSpec((R * KB, KB * R), lambda b, k, g: (0, 0)),
                  pl.BlockSpec((R * KB, KB * R), lambda b, k, g: (0, 0)),
                  pl.BlockSpec((1, KB * R, 2 * gd), lambda b, k, g: (b, k, g))],
        out_specs=pl.BlockSpec((1, R, KB, gd), lambda b, k, g: (b, 0, k, g)),
        out_shape=jax.ShapeDtypeStruct((cfg.DEC_BATCH, R, R, D), bf16),
        compiler_params=_cparams(("arbitrary", "arbitrary", "arbitrary")),
        name="fft_stage2",
    )(k2c, k2s, t3)
    return out.reshape(cfg.DEC_BATCH * cfg.seg_len, D)


def _post_call(fs, w_out, y, modr, layer, which, ln_g, ln_b, cfg):
    N, D, tm = cfg.N, cfg.D, cfg.TM
    spt = cfg.seg_len // tm
    two = len(fs) == 2
    alpha = cfg.alpha

    def kern(*refs):
        if two:
            fp_ref, fs_ref, w_ref, y_ref, g_ref, lg_ref, lb_ref, o_ref = refs
            f = jnp.where(pl.program_id(0) < spt, fp_ref[...], fs_ref[...])
        else:
            f_ref, w_ref, y_ref, g_ref, lg_ref, lb_ref, o_ref = refs
            f = f_ref[...]
        m = _dot(f, w_ref[...])
        z = alpha * y_ref[...] + g_ref[0] * m
        o_ref[...] = _ln(z) * lg_ref[...] + lb_ref[...]

    if two:
        f_specs = [pl.BlockSpec((tm, D), lambda i: (jnp.minimum(i, spt - 1), 0)),
                   pl.BlockSpec((tm, D), lambda i: (jnp.maximum(i - spt, 0), 0))]
    else:
        f_specs = [pl.BlockSpec((tm, D), lambda i: (i, 0))]
    return pl.pallas_call(
        kern,
        grid=(N // tm,),
        in_specs=f_specs + [pl.BlockSpec((D, D), lambda i: (0, 0)),
                            pl.BlockSpec((tm, D), lambda i: (i, 0)),
                            _mod_spec(cfg, layer, which, tm),
                            pl.BlockSpec((1, D), lambda i: (0, 0)),
                            pl.BlockSpec((1, D), lambda i: (0, 0))],
        out_specs=pl.BlockSpec((tm, D), lambda i: (i, 0)),
        out_shape=jax.ShapeDtypeStruct((N, D), f32),
        compiler_params=_cparams(("arbitrary",)),
        name="mixer_post",
    )(*fs, w_out, y, modr, ln_g.reshape(1, D), ln_b.reshape(1, D))


def _mla_a_call(y, modr, layer, w_a2, g_q, g_kv, cos_t, sin_t, cfg):
    N, D, tm = cfg.N, cfg.D, cfg.TM
    QL, KVL, RP = cfg.QL, cfg.KVL, cfg.ROPE
    wa = w_a2.shape[1]
    o_kr = QL + KVL
    o_rot = o_kr + 128

    def rms(x, g):
        return x * lax.rsqrt(jnp.mean(x * x, axis=-1, keepdims=True) + RMS_EPS) * g

    def kern(y_ref, sh_ref, sc_ref, w_ref, gq_ref, gkv_ref, cos_ref, sin_ref,
             cq_ref, ckv_ref, kr_ref, krr_ref):
        u = _ln(y_ref[...]) * (1.0 + sc_ref[0]) + sh_ref[0]
        a = _dot(u.astype(bf16), w_ref[...])
        cq_ref[...] = rms(a[:, :QL], gq_ref[...]).astype(bf16)
        ckv_ref[...] = rms(a[:, QL:QL + KVL], gkv_ref[...])
        kr = a[:, o_kr:o_kr + RP]
        kr_ref[...] = kr
        krr_ref[...] = kr * cos_ref[...] + a[:, o_rot:o_rot + RP] * sin_ref[...]

    return pl.pallas_call(
        kern,
        grid=(N // tm,),
        in_specs=[pl.BlockSpec((tm, D), lambda i: (i, 0)),
                  _mod_spec(cfg, layer, 0, tm), _mod_spec(cfg, layer, 1, tm),
                  pl.BlockSpec((D, wa), lambda i: (0, 0)),
                  pl.BlockSpec((1, QL), lambda i: (0, 0)),
                  pl.BlockSpec((1, KVL), lambda i: (0, 0)),
                  pl.BlockSpec((tm, RP), lambda i: (i, 0)),
                  pl.BlockSpec((tm, RP), lambda i: (i, 0))],
        out_specs=[pl.BlockSpec((tm, QL), lambda i: (i, 0)),
                   pl.BlockSpec((tm, KVL), lambda i: (i, 0)),
                   pl.BlockSpec((tm, RP), lambda i: (i, 0)),
                   pl.BlockSpec((tm, RP), lambda i: (i, 0))],
        out_shape=[jax.ShapeDtypeStruct((N, QL), bf16),
                   jax.ShapeDtypeStruct((N, KVL), f32),
                   jax.ShapeDtypeStruct((N, RP), f32),
                   jax.ShapeDtypeStruct((N, RP), f32)],
        compiler_params=_cparams(("arbitrary",)),
        name="mla_down",
    )(y, modr, modr, w_a2, g_q.reshape(1, QL), g_kv.reshape(1, KVL), cos_t, sin_t)


def _q_call(cq, w_q, cos_t, sin_t, cfg):
    N, tm, H = cfg.N, cfg.TM, cfg.H
    QL, NP, RP = cfg.QL, cfg.NOPE, cfg.ROPE
    dk = NP + RP
    wq = NP + 2 * RP
    qscale = LOG2E / math.sqrt(dk)

    def kern(cq_ref, w_ref, cos_ref, sin_ref, q_ref):
        cq_t = cq_ref[...]
        cos, sin = cos_ref[...], sin_ref[...]
        for h in range(H):
            t = _dot(cq_t, w_ref[h])
            rope = t[:, NP:NP + RP] * cos + t[:, NP + RP:] * sin
            q_ref[h] = (jnp.concatenate([t[:, :NP], rope], axis=-1) * qscale).astype(bf16)

    return pl.pallas_call(
        kern,
        grid=(N // tm,),
        in_specs=[pl.BlockSpec((tm, QL), lambda i: (i, 0)),
                  pl.BlockSpec((H, QL, wq), lambda i: (0, 0, 0)),
                  pl.BlockSpec((tm, RP), lambda i: (i, 0)),
                  pl.BlockSpec((tm, RP), lambda i: (i, 0))],
        out_specs=pl.BlockSpec((H, tm, dk), lambda i: (0, i, 0)),
        out_shape=jax.ShapeDtypeStruct((H, N, dk), bf16),
        compiler_params=_cparams(("arbitrary",)),
        name="mla_q",
    )(cq, w_q, cos_t, sin_t)


def _kv_call(ckv_all, kr_all, w_kv, cfg):
    H, KVL, NP, RP, VD = cfg.H, cfg.KVL, cfg.NOPE, cfg.ROPE, cfg.VD
    LK, tkv, ns = cfg.LK, cfg.TKV, cfg.n_seg
    dk = NP + RP

    def kern(c_ref, kr_ref, w_ref, k_ref, v_ref):
        c = c_ref[0].astype(bf16)
        kr = kr_ref[0]
        for h in range(H):
            t = _dot(c, w_ref[h])
            k_ref[h, 0] = jnp.concatenate([t[:, :NP], kr], axis=-1).astype(bf16)
            v_ref[h, 0] = jnp.concatenate([t[:, NP:], jnp.ones((tkv, 128), f32)], axis=-1).astype(bf16)

    return pl.pallas_call(
        kern,
        grid=(ns, LK // tkv),
        in_specs=[pl.BlockSpec((1, tkv, KVL), lambda s, i: (s, i, 0)),
                  pl.BlockSpec((1, tkv, RP), lambda s, i: (s, i, 0)),
                  pl.BlockSpec((H, KVL, NP + VD), lambda s, i: (0, 0, 0))],
        out_specs=[pl.BlockSpec((H, 1, tkv, dk), lambda s, i: (0, s, i, 0)),
                   pl.BlockSpec((H, 1, tkv, VD + 128), lambda s, i: (0, s, i, 0))],
        out_shape=[jax.ShapeDtypeStruct((H, ns, LK, dk), bf16),
                   jax.ShapeDtypeStruct((H, ns, LK, VD + 128), bf16)],
        compiler_params=_cparams(("arbitrary", "arbitrary")),
        name="mla_kv",
    )(ckv_all, kr_all, w_kv)


def _attention_call(q, k, v, cfg):
    H, NP, RP, VD = cfg.H, cfg.NOPE, cfg.ROPE, cfg.VD
    assert VD == 128
    dk = NP + RP
    LK, TQ, SEQ, HB = cfg.LK, cfg.TQ, cfg.SEQ, cfg.HB
    nq = cfg.seg_len // TQ
    sub = TQ // SEQ

    def scores(qb, kb):
        return lax.dot_general(qb, kb, (((1,), (1,)), ((), ())), preferred_element_type=f32)

    def finish(s, vb):
        p = jnp.exp2(s - jnp.max(s, axis=-1, keepdims=True)).astype(bf16)
        acc = _dot(p, vb)
        return (acc[:, :VD] / acc[:, VD:]).astype(bf16)

    def kern(q_ref, k_ref, v_ref, o_ref):
        seg = pl.program_id(0)
        qi = pl.program_id(2)

        @pl.when(seg == 0)
        def _():
            for hh in range(HB):
                for j in range(sub):
                    off = pl.multiple_of((qi * sub + j) * SEQ, SEQ)
                    s = scores(q_ref[hh, j * SEQ:(j + 1) * SEQ, :], k_ref[hh, 0, pl.ds(off, SEQ), :])
                    o_ref[j * SEQ:(j + 1) * SEQ, hh * VD:(hh + 1) * VD] = finish(
                        s, v_ref[hh, 0, pl.ds(off, SEQ), :])

        @pl.when(seg != 0)
        def _():
            ss = [scores(q_ref[hh], k_ref[hh, 0]) for hh in range(HB)]
            for hh in range(HB):
                o_ref[:, hh * VD:(hh + 1) * VD] = finish(ss[hh], v_ref[hh, 0])

    return pl.pallas_call(
        kern,
        grid=(cfg.n_seg, H // HB, nq),
        in_specs=[pl.BlockSpec((HB, TQ, dk), lambda s, h, i: (h, s * nq + i, 0)),
                  pl.BlockSpec((HB, 1, LK, dk), lambda s, h, i: (h, s, 0, 0)),
                  pl.BlockSpec((HB, 1, LK, VD + 128), lambda s, h, i: (h, s, 0, 0))],
        out_specs=pl.BlockSpec((TQ, HB * VD), lambda s, h, i: (s * nq + i, h)),
        out_shape=jax.ShapeDtypeStruct((cfg.N, H * VD), bf16),
        compiler_params=_cparams(("arbitrary", "arbitrary", "arbitrary")),
        name="mla_attention",
    )(q, k, v)


def _router_call(y, modr, layer, wr2, b_router, cfg):
    N, D, tm, E, NG = cfg.N, cfg.D, cfg.TM, cfg.E, cfg.NG
    epg = E // NG

    def top2(x, jio):
        m1 = jnp.max(x, axis=0, keepdims=True)
        i1 = jnp.min(jnp.where(x == m1, jio, epg), axis=0, keepdims=True)
        x2 = jnp.where(jio == i1, -jnp.inf, x)
        m2 = jnp.max(x2, axis=0, keepdims=True)
        i2 = jnp.min(jnp.where(x2 == m2, jio, epg), axis=0, keepdims=True)
        return m1, i1, m2, i2

    def kern(y_ref, sh_ref, sc_ref, w_ref, b_ref, u_ref, e_ref, g_ref, cnt_ref):
        i = pl.program_id(0)
        u = _ln(y_ref[...]) * (1.0 + sc_ref[0]) + sh_ref[0]
        u_ref[...] = u
        u_hi = u.astype(bf16)
        u_lo = (u - u_hi.astype(f32)).astype(bf16)
        out = _dot(jnp.concatenate([u_hi, u_lo], axis=1), w_ref[...])
        lt = out.T
        logits = lt[:E] + lt[E:]
        scores = _sigmoid(logits)
        biased = scores + b_ref[...]
        jio = lax.broadcasted_iota(i32, (epg, tm), 0)
        best = gi = None
        for g in range(NG):
            m1, _, m2, _ = top2(biased[g * epg:(g + 1) * epg], jio)
            gs = m1 + m2
            if g == 0:
                best, gi = gs, jnp.zeros((1, tm), i32)
            else:
                better = gs > best
                best = jnp.where(better, gs, best)
                gi = jnp.where(better, g, gi)
        sel_b = biased[:epg]
        sel_s = scores[:epg]
        for g in range(1, NG):
            pick = gi == g
            sel_b = jnp.where(pick, biased[g * epg:(g + 1) * epg], sel_b)
            sel_s = jnp.where(pick, scores[g * epg:(g + 1) * epg], sel_s)
        _, i1, _, i2 = top2(sel_b, jio)
        s1 = jnp.sum(jnp.where(jio == i1, sel_s, 0.0), axis=0, keepdims=True)
        s2 = jnp.sum(jnp.where(jio == i2, sel_s, 0.0), axis=0, keepdims=True)
        den = s1 + s2
        e1 = gi * epg + i1
        e2 = gi * epg + i2
        e_ref[0:1, :] = e1
        e_ref[1:2, :] = e2
        rio = lax.broadcasted_iota(i32, (128, tm), 0)
        gates = jnp.where(rio == 0, ROUTED_SCALE * s1 / den,
                          jnp.where(rio == 1, ROUTED_SCALE * s2 / den, 0.0))
        g_ref[...] = gates.T
        eio = lax.broadcasted_iota(i32, (E, tm), 0)
        hits = jnp.where(eio == e1, 1.0, 0.0) + jnp.where(eio == e2, 1.0, 0.0)
        part = jnp.sum(hits, axis=1, keepdims=True)

        @pl.when(i == 0)
        def _():
            cnt_ref[...] = jnp.zeros_like(cnt_ref)

        cnt_ref[...] += part

    return pl.pallas_call(
        kern,
        grid=(N // tm,),
        in_specs=[pl.BlockSpec((tm, D), lambda i: (i, 0)),
                  _mod_spec(cfg, layer, 3, tm), _mod_spec(cfg, layer, 4, tm),
                  pl.BlockSpec((2 * D, 2 * E), lambda i: (0, 0)),
                  pl.BlockSpec((E, 1), lambda i: (0, 0))],
        out_specs=[pl.BlockSpec((tm, D), lambda i: (i, 0)),
                   pl.BlockSpec((2, tm), lambda i: (0, i)),
                   pl.BlockSpec((tm, 128), lambda i: (i, 0)),
                   pl.BlockSpec((E, 128), lambda i: (0, 0))],
        out_shape=[jax.ShapeDtypeStruct((N, D), f32),
                   jax.ShapeDtypeStruct((2, N), i32),
                   jax.ShapeDtypeStruct((N, 128), f32),
                   jax.ShapeDtypeStruct((E, 128), f32)],
        compiler_params=_cparams(("arbitrary",)),
        name="moe_router",
    )(y, modr, modr, wr2, b_router.reshape(E, 1))


def _rank_call(e_flat, pstart, tri, cfg):
    E, C = cfg.E, cfg.RC
    n_asg = e_flat.shape[1]

    def kern(e_ref, ps_ref, tri_ref, d_ref, base_ref):
        @pl.when(pl.program_id(0) == 0)
        def _():
            base_ref[...] = jnp.zeros_like(base_ref)

        eio = lax.broadcasted_iota(i32, (E, C), 0)
        oh = eio == e_ref[...]
        incl = _dot(jnp.where(oh, 1.0, 0.0).astype(bf16), tri_ref[...])
        base = base_ref[:, 0:1]
        pos = ps_ref[...] + base + incl - 1.0
        d_ref[...] = jnp.sum(jnp.where(oh, pos, 0.0), axis=0, keepdims=True).astype(i32)
        base_ref[...] += incl[:, C - 1:C]

    return pl.pallas_call(
        kern,
        grid=(n_asg // C,),
        in_specs=[pl.BlockSpec((1, C), lambda i: (0, i)),
                  pl.BlockSpec((E, 1), lambda i: (0, 0)),
                  pl.BlockSpec((C, C), lambda i: (0, 0))],
        out_specs=pl.BlockSpec((1, C), lambda i: (0, i)),
        out_shape=jax.ShapeDtypeStruct((1, n_asg), i32),
        scratch_shapes=[pltpu.VMEM((E, 128), f32)],
        compiler_params=_cparams(("arbitrary",)),
        name="moe_rank",
    )(e_flat, pstart, tri)


def _dispatch_call(dest3, u, xs0, cfg):
    N, D, tm = cfg.N, cfg.D, cfg.TM

    def kern(d_ref, u_ref, xs_in_ref, xs_ref, sem):
        del xs_in_ref

        def issue(r, c):
            for k in range(2):
                pltpu.make_async_copy(u_ref.at[pl.ds(r, 1), :],
                                      xs_ref.at[pl.ds(d_ref[0, k, r], 1), :], sem).start()
            return c

        lax.fori_loop(0, tm, issue, 0, unroll=8)
        for k in range(2):
            pltpu.make_async_copy(u_ref, xs_ref.at[pl.ds(0, tm), :], sem).wait()

    return pl.pallas_call(
        kern,
        grid=(N // tm,),
        in_specs=[pl.BlockSpec((1, 2, tm), lambda i: (i, 0, 0), memory_space=pltpu.SMEM),
                  pl.BlockSpec((tm, D), lambda i: (i, 0)),
                  pl.BlockSpec(memory_space=pl.ANY)],
        out_specs=pl.BlockSpec(memory_space=pl.ANY),
        out_shape=jax.ShapeDtypeStruct(xs0.shape, xs0.dtype),
        scratch_shapes=[pltpu.SemaphoreType.DMA(())],
        input_output_aliases={2: 0},
        compiler_params=_cparams(("arbitrary",)),
        name="moe_dispatch",
    )(dest3, u, xs0)


def _expert_call(tile_e, tile_src, tile_used, xs, w_exp_in, w_exp_out, layer, cfg):
    D, DE, TE = cfg.D, cfg.DE, cfg.TE
    n_rows = xs.shape[0]
    n_tiles = n_rows // TE

    def kern(te_ref, ts_ref, tu_ref, x_ref, wi_ref, wo_ref, y_ref, wi_bf, wo_bf):
        i = pl.program_id(0)
        changed = te_ref[i] != te_ref[jnp.maximum(i - 1, 0)]

        @pl.when((i == 0) | changed)
        def _():
            wi_bf[...] = wi_ref[0, 0].astype(bf16)
            wo_bf[...] = wo_ref[0, 0].astype(bf16)

        @pl.when(tu_ref[i] == 1)
        def _():
            h = _dot(x_ref[...].astype(bf16), wi_bf[...])
            act = h[:, :DE] * _sigmoid(h[:, :DE]) * h[:, DE:]
            y_ref[...] = _dot(act.astype(bf16), wo_bf[...])

        @pl.when(tu_ref[i] == 0)
        def _():
            y_ref[...] = jnp.zeros_like(y_ref)

    grid_spec = pltpu.PrefetchScalarGridSpec(
        num_scalar_prefetch=3,
        grid=(n_tiles,),
        in_specs=[pl.BlockSpec((TE, D), lambda i, te, ts, tu: (ts[i], 0)),
                  pl.BlockSpec((1, 1, D, 2 * DE), lambda i, te, ts, tu: (layer, te[i], 0, 0)),
                  pl.BlockSpec((1, 1, DE, D), lambda i, te, ts, tu: (layer, te[i], 0, 0))],
        out_specs=pl.BlockSpec((TE, D), lambda i, te, ts, tu: (i, 0)),
        scratch_shapes=[pltpu.VMEM((D, 2 * DE), bf16), pltpu.VMEM((DE, D), bf16)],
    )
    return pl.pallas_call(
        kern,
        grid_spec=grid_spec,
        out_shape=jax.ShapeDtypeStruct((n_rows, D), f32),
        compiler_params=_cparams(("arbitrary",)),
        name="moe_experts",
    )(tile_e, tile_src, tile_used, xs, w_exp_in, w_exp_out)


def _combine_call(dest3, y, gt, ys, modr, layer, ln_g, ln_b, cfg):
    N, D, tm = cfg.N, cfg.D, cfg.TM
    alpha = cfg.alpha

    def kern(d_ref, y_ref, gt_ref, g2_ref, lg_ref, lb_ref, ys_ref, o_ref, r0, r1, sem):
        bufs = (r0, r1)

        def issue(r, c):
            for k in range(2):
                pltpu.make_async_copy(ys_ref.at[pl.ds(d_ref[0, k, r], 1), :],
                                      bufs[k].at[pl.ds(r, 1), :], sem).start()
            return c

        lax.fori_loop(0, tm, issue, 0, unroll=8)
        for k in range(2):
            pltpu.make_async_copy(ys_ref.at[pl.ds(0, tm), :], bufs[k], sem).wait()
        gt_t = gt_ref[...]
        f = gt_t[:, 0:1] * r0[...] + gt_t[:, 1:2] * r1[...]
        z = alpha * y_ref[...] + g2_ref[0] * f
        o_ref[...] = _ln(z) * lg_ref[...] + lb_ref[...]

    return pl.pallas_call(
        kern,
        grid=(N // tm,),
        in_specs=[pl.BlockSpec((1, 2, tm), lambda i: (i, 0, 0), memory_space=pltpu.SMEM),
                  pl.BlockSpec((tm, D), lambda i: (i, 0)),
                  pl.BlockSpec((tm, 128), lambda i: (i, 0)),
                  _mod_spec(cfg, layer, 5, tm),
                  pl.BlockSpec((1, D), lambda i: (0, 0)),
                  pl.BlockSpec((1, D), lambda i: (0, 0)),
                  pl.BlockSpec(memory_space=pl.ANY)],
        out_specs=pl.BlockSpec((tm, D), lambda i: (i, 0)),
        out_shape=jax.ShapeDtypeStruct((N, D), f32),
        scratch_shapes=[pltpu.VMEM((tm, D), f32), pltpu.VMEM((tm, D), f32),
                        pltpu.SemaphoreType.DMA(())],
        compiler_params=_cparams(("arbitrary",)),
        name="moe_combine",
    )(dest3, y, gt, modr, ln_g.reshape(1, D), ln_b.reshape(1, D), ys)


def _moe_layer(y, xs_buf, modr, layer, wr2, b_router, tri, w_exp_in, w_exp_out, ln_g, ln_b, cfg):
    N, D, E, TE, tm = cfg.N, cfg.D, cfg.E, cfg.TE, cfg.TM
    u, e2, gt, cnt = _router_call(y, modr, layer, wr2, b_router, cfg)
    counts = cnt[:, 0].astype(i32)
    padded = (counts + TE - 1) // TE * TE
    pend = jnp.cumsum(padded)
    pstart = pend - padded
    n_rows = 2 * N + E * TE
    n_tiles = n_rows // TE
    tile_start = jnp.arange(n_tiles, dtype=i32) * TE
    n_used = pend[-1] // TE
    tile_src = jnp.minimum(jnp.arange(n_tiles, dtype=i32), n_used - 1)
    tile_e = jnp.minimum(jnp.sum((pend[None, :] <= (tile_src * TE)[:, None]).astype(i32), axis=1), E - 1)
    tile_used = (tile_start < pend[-1]).astype(i32)

    dest = _rank_call(e2.reshape(1, 2 * N), pstart.astype(f32).reshape(E, 1), tri, cfg)
    dest3 = dest.reshape(2, N // tm, tm).transpose(1, 0, 2)
    xs = _dispatch_call(dest3, u, xs_buf, cfg)
    ys = _expert_call(tile_e, tile_src, tile_used, xs, w_exp_in, w_exp_out, layer, cfg)
    return _combine_call(dest3, y, gt, ys, modr, layer, ln_g, ln_b, cfg), xs


def _dft_tables(cfg):
    D, R, L = cfg.D, cfg.R, cfg.SEQ
    gd = D // cfg.F_GROUPS

    def cs(n, scale):
        k = np.arange(n)
        ang = 2.0 * np.pi * ((k[:, None] * k[None, :]) % n) / n
        return np.cos(ang) * scale, np.sin(ang) * scale

    cc, sc = cs(gd, 1.0 / math.sqrt(gd))
    chan = jnp.asarray(np.concatenate([cc, -sc], axis=1), bf16)
    cl, sl = cs(L, 1.0 / math.sqrt(L))
    cr, sr = cs(R, 1.0 / math.sqrt(R))
    J, KB = cfg.FFT_J, cfg.FFT_KB
    f2 = np.concatenate([cr, -sr], axis=0)
    k1m = jnp.asarray(np.kron(f2, np.eye(J)), bf16)
    eye = np.eye(KB)
    k2c = jnp.asarray(np.einsum("kn,ab->kabn", cr, eye).reshape(R * KB, KB * R), bf16)
    k2s = jnp.asarray(np.einsum("kn,ab->kabn", sr, eye).reshape(R * KB, KB * R), bf16)
    k1 = np.arange(R)
    ang = 2.0 * np.pi * (k1[:, None] * k1[None, :]) / (R * R)
    ang = ang.reshape(R, R // J, J).transpose(1, 0, 2).reshape(R // J, R * J)
    tw_re = jnp.asarray(np.repeat(np.cos(ang)[:, :, None], 128, axis=2), f32)
    tw_im = jnp.asarray(np.repeat(-np.sin(ang)[:, :, None], 128, axis=2), f32)
    return dict(chan=chan, c_seq=jnp.asarray(cl, bf16), s_seq=jnp.asarray(sl, bf16), k1m=k1m,
                k2c=k2c, k2s=k2s, tw_re=tw_re, tw_im=tw_im)


def _rot_cols(w):
    shp = w.shape
    w2 = w.reshape(shp[:-1] + (shp[-1] // 2, 2))
    return jnp.stack([-w2[..., 1], w2[..., 0]], axis=-1).reshape(shp)


def _rope_tables(cfg):
    L, RP = cfg.seg_len, cfg.ROPE
    pairs = RP // 4
    rows = L // cfg.GRID_W
    t_row = jnp.repeat(jnp.arange(rows, dtype=f32), cfg.GRID_W)
    t_col = jnp.tile(jnp.arange(cfg.GRID_W, dtype=f32), rows)
    inv = ROPE_BASE ** (-jnp.arange(pairs, dtype=f32) / pairs)
    ang = jnp.concatenate([t_row[:, None] * inv, t_col[:, None] * inv], -1)
    cos = jnp.repeat(jnp.cos(ang), 2, axis=-1)
    sin = jnp.repeat(jnp.sin(ang), 2, axis=-1)
    cos_t = jnp.concatenate([jnp.ones((L, RP), f32), jnp.tile(cos, (cfg.DEC_BATCH, 1))], 0)
    sin_t = jnp.concatenate([jnp.zeros((L, RP), f32), jnp.tile(sin, (cfg.DEC_BATCH, 1))], 0)
    return cos_t, sin_t


def _mla_weights(w_a, w_uq, w_ukv, cfg):
    D, H, QL, KVL, NP, RP, VD = cfg.D, cfg.H, cfg.QL, cfg.KVL, cfg.NOPE, cfg.ROPE, cfg.VD
    kr_w = w_a[:, QL + KVL:]
    z = jnp.zeros((D, 128 - RP), f32)
    w_a2 = jnp.concatenate([w_a[:, :QL + KVL], kr_w, z, _rot_cols(kr_w), z], axis=1).astype(bf16)
    wq = w_uq.reshape(QL, H, NP + RP)
    w_q = jnp.concatenate([wq, _rot_cols(wq[..., NP:])], axis=-1).transpose(1, 0, 2).astype(bf16)
    w_kv = w_ukv.reshape(KVL, H, NP + VD).transpose(1, 0, 2).astype(bf16)
    return w_a2, w_q, w_kv


def _forward(cfg, x_prompt, x_sample, c, cache_ckv, cache_krope, c_ctx, w_ada, b_ada, w_f_in, w_f_out,
             w_mla_a, g_mla_q, g_mla_kv, w_mla_uq, w_mla_ukv, w_mla_o, ln_mix_g, ln_mix_b,
             ln_ffn_g, ln_ffn_b, w_router, b_router, w_exp_in, w_exp_out):
    assert cfg.BATCH * cfg.SEQ == cfg.seg_len and cfg.R * cfg.R == cfg.seg_len
    assert cfg.TQ % cfg.SEQ == 0 and cfg.LK % cfg.TKV == 0
    assert cfg.R % cfg.FFT_J == 0 and cfg.R % cfg.FFT_KB == 0
    N, D, L, E = cfg.N, cfg.D, cfg.seg_len, cfg.E
    y = jnp.concatenate([x_prompt.reshape(L, D), x_sample.reshape(cfg.DEC_BATCH * L, D)], axis=0)

    cvec = jnp.concatenate([c_ctx[None], c, jnp.zeros((MOD_ROWS - cfg.n_seg, D), f32)], axis=0)
    mod = _ada_call(cvec, w_ada, b_ada, cfg)
    modr = mod.reshape(cfg.DEPTH * MOD_ROWS * 6, 1, D)

    tabs = _dft_tables(cfg)
    cos_t, sin_t = _rope_tables(cfg)
    w_hi = w_router.astype(bf16)
    w_lo = (w_router - w_hi.astype(f32)).astype(bf16)
    wr2 = jnp.concatenate([jnp.concatenate([w_hi, w_lo], 1),
                           jnp.concatenate([w_hi, jnp.zeros_like(w_lo)], 1)], 0)
    rc = cfg.RC
    tri = jnp.asarray(np.triu(np.ones((rc, rc), np.float32)), bf16)

    ckv_out, kr_out = [], []
    xs_buf = jnp.zeros((2 * N + E * cfg.TE, D), f32)
    for l in range(cfg.DEPTH):
        j = l // 2
        if l % 2 == 0:
            a = _fourier_in_call(y, modr, l, w_f_in[j].astype(bf16), tabs["chan"], cfg)
            f_p = _prompt_dft_call(a, tabs["c_seq"], tabs["s_seq"], cfg)
            t = _fft_stage1_call(a, tabs["k1m"], tabs["tw_re"], tabs["tw_im"], cfg)
            f_s = _fft_stage2_call(t, tabs["k2c"], tabs["k2s"], cfg)
            y = _post_call((f_p, f_s), w_f_out[j].astype(bf16), y, modr, l, 2, ln_mix_g[l], ln_mix_b[l], cfg)
        else:
            w_a2, w_q, w_kv = _mla_weights(w_mla_a[j], w_mla_uq[j], w_mla_ukv[j], cfg)
            cq, ckv, kr, krr = _mla_a_call(y, modr, l, w_a2, g_mla_q[j], g_mla_kv[j], cos_t, sin_t, cfg)
            ckv_out.append(ckv[:L].reshape(cfg.BATCH, cfg.SEQ, cfg.KVL))
            kr_out.append(kr[:L].reshape(cfg.BATCH, cfg.SEQ, cfg.ROPE))
            q = _q_call(cq, w_q, cos_t, sin_t, cfg)
            pad = cfg.PAST
            ckv_all = jnp.concatenate([
                jnp.concatenate([ckv[:L], jnp.zeros((pad, cfg.KVL), f32)], 0)[None],
                jnp.concatenate([cache_ckv[:, j], ckv[L:].reshape(cfg.DEC_BATCH, L, cfg.KVL)], 1)], 0)
            kr_all = jnp.concatenate([
                jnp.concatenate([krr[:L], jnp.zeros((pad, cfg.ROPE), f32)], 0)[None],
                jnp.concatenate([cache_krope[:, j], krr[L:].reshape(cfg.DEC_BATCH, L, cfg.ROPE)], 1)], 0)
            k, v = _kv_call(ckv_all, kr_all, w_kv, cfg)
            o = _attention_call(q, k, v, cfg)
            y = _post_call((o,), w_mla_o[j].astype(bf16), y, modr, l, 2, ln_mix_g[l], ln_mix_b[l], cfg)
        y, xs_buf = _moe_layer(y, xs_buf, modr, l, wr2, b_router, tri, w_exp_in, w_exp_out, ln_ffn_g[l], ln_ffn_b[l], cfg)

    y_p = y[:L].reshape(cfg.BATCH, cfg.SEQ, D)
    y_s = y[L:].reshape(cfg.DEC_BATCH, L, D)
    return y_p, y_s, jnp.stack(ckv_out, axis=1), jnp.stack(kr_out, axis=1)


def kernel(x_prompt, x_sample, c, cache_ckv, cache_krope, c_ctx, w_ada, b_ada, w_f_in, w_f_out, w_mla_a, g_mla_q, g_mla_kv, w_mla_uq, w_mla_ukv, w_mla_o, ln_mix_g, ln_mix_b, ln_ffn_g, ln_ffn_b, w_router, b_router, w_exp_in, w_exp_out):
    return _forward(Cfg(), x_prompt, x_sample, c, cache_ckv, cache_krope, c_ctx, w_ada, b_ada, w_f_in,
                    w_f_out, w_mla_a, g_mla_q, g_mla_kv, w_mla_uq, w_mla_ukv, w_mla_o, ln_mix_g,
                    ln_mix_b, ln_ffn_g, ln_ffn_b, w_router, b_router, w_exp_in, w_exp_out)
```

```python
import functools
import math
from typing import NamedTuple

import numpy as np
import jax
import jax.numpy as jnp
from jax import lax
from jax.experimental import pallas as pl
from jax.experimental.pallas import tpu as pltpu

f32 = jnp.float32
bf16 = jnp.bfloat16
i32 = jnp.int32

LN_EPS = 1e-5
RMS_EPS = 1e-6
ROPE_BASE = 10000.0
ROUTED_SCALE = 2.5
LOG2E = 1.4426950408889634
NEG_BIG = -1e30
MOD_ROWS = 16
VMEM_LIMIT_V7X = 56 * 1024 * 1024


class Cfg(NamedTuple):
    D: int = 2048
    DEPTH: int = 4
    BATCH: int = 16
    SEQ: int = 256
    DEC_BATCH: int = 8
    DEC_SEQ: int = 4096
    PAST: int = 256
    GRID_W: int = 64
    F_GROUPS: int = 4
    H: int = 16
    QL: int = 512
    KVL: int = 512
    NOPE: int = 128
    ROPE: int = 64
    VD: int = 128
    E: int = 64
    NG: int = 8
    DE: int = 512
    TM: int = 512
    TN_ADA: int = 1024
    FFT_J: int = 16
    FFT_KB: int = 16
    TKV: int = 256
    TQ: int = 512
    HB: int = 2
    RC: int = 1024
    TE: int = 256

    @property
    def seg_len(self):
        return self.DEC_SEQ

    @property
    def n_seg(self):
        return 1 + self.DEC_BATCH

    @property
    def N(self):
        return self.n_seg * self.seg_len

    @property
    def R(self):
        return int(round(math.sqrt(self.seg_len)))

    @property
    def LK(self):
        return self.PAST + self.seg_len

    @property
    def alpha(self):
        return (2 * self.DEPTH) ** 0.25


def _cparams(sem):
    return pltpu.CompilerParams(dimension_semantics=sem, vmem_limit_bytes=VMEM_LIMIT_V7X)


def _ln(x):
    mu = jnp.mean(x, axis=-1, keepdims=True)
    xc = x - mu
    var = jnp.mean(xc * xc, axis=-1, keepdims=True)
    return xc * lax.rsqrt(var + LN_EPS)


def _sigmoid(x):
    return 1.0 / (1.0 + jnp.exp(-x))


def _dot(a, b):
    return jnp.dot(a, b, preferred_element_type=f32)


def _mod_spec(cfg, layer, which, tm):
    spt = cfg.seg_len // tm
    base = layer * MOD_ROWS * 6
    return pl.BlockSpec((1, 1, cfg.D), lambda i: (base + (i // spt) * 6 + which, 0, 0))


def _rows_specs(n_arrs, tm, cfg, width):
    spt = cfg.seg_len // tm
    if n_arrs == 2:
        return [pl.BlockSpec((tm, width), lambda i: (jnp.minimum(i, spt - 1), 0)),
                pl.BlockSpec((tm, width), lambda i: (jnp.maximum(i - spt, 0), 0))]
    return [pl.BlockSpec((tm, width), lambda i: (i, 0))]


def _rows_read(refs, spt):
    if len(refs) == 2:
        return jnp.where(pl.program_id(0) < spt, refs[0][...], refs[1][...])
    return refs[0][...]


def _ada_call(cvec, w_ada, b_ada, cfg):
    D, n6, tn = cfg.D, 6 * cfg.D, cfg.TN_ADA

    def kern(c_ref, w_ref, b_ref, o_ref):
        c = c_ref[...]
        s = (c * _sigmoid(c)).astype(bf16)
        o_ref[0] = _dot(s, w_ref[0].astype(bf16)) + b_ref[0]

    return pl.pallas_call(
        kern,
        grid=(cfg.DEPTH, n6 // tn),
        in_specs=[pl.BlockSpec((MOD_ROWS, D), lambda l, j: (0, 0)),
                  pl.BlockSpec((1, D, tn), lambda l, j: (l, 0, j)),
                  pl.BlockSpec((1, 1, tn), lambda l, j: (l, 0, j))],
        out_specs=pl.BlockSpec((1, MOD_ROWS, tn), lambda l, j: (l, 0, j)),
        out_shape=jax.ShapeDtypeStruct((cfg.DEPTH, MOD_ROWS, n6), f32),
        compiler_params=_cparams(("arbitrary", "arbitrary")),
        name="ada",
    )(cvec, w_ada, b_ada.reshape(cfg.DEPTH, 1, n6))


def _fourier_in_call(y, modr, layer, w_in, cs, cfg):
    N, D, tm = cfg.N, cfg.D, cfg.TM
    gd = D // cfg.F_GROUPS
    ny = len(y)
    spt = cfg.seg_len // tm

    def kern(*refs):
        y_refs, (sh_ref, sc_ref, w_ref, cs_ref, a_ref) = refs[:ny], refs[ny:]
        u = _ln(_rows_read(y_refs, spt)) * (1.0 + sc_ref[0]) + sh_ref[0]
        h = _dot(u.astype(bf16), w_ref[...]).astype(bf16)
        for g in range(cfg.F_GROUPS):
            a = _dot(h[:, g * gd:(g + 1) * gd], cs_ref[...])
            a_ref[:, g * 2 * gd:(g + 1) * 2 * gd] = a.astype(bf16)

    return pl.pallas_call(
        kern,
        grid=(N // tm,),
        in_specs=_rows_specs(ny, tm, cfg, D) + [
            _mod_spec(cfg, layer, 0, tm), _mod_spec(cfg, layer, 1, tm),
            pl.BlockSpec((D, D), lambda i: (0, 0)),
            pl.BlockSpec((gd, 2 * gd), lambda i: (0, 0))],
        out_specs=pl.BlockSpec((tm, 2 * D), lambda i: (i, 0)),
        out_shape=jax.ShapeDtypeStruct((N, 2 * D), bf16),
        compiler_params=_cparams(("arbitrary",)),
        name="fourier_in",
    )(*y, modr, modr, w_in, cs)


def _prompt_dft_call(a, c_seq, s_seq, cfg):
    D, L = cfg.D, cfg.SEQ
    gd = D // cfg.F_GROUPS
    a3 = a.reshape(cfg.N // L, L, 2 * D)

    def kern(c_ref, s_ref, x_ref, o_ref):
        x = x_ref[0]
        o_ref[0] = (_dot(c_ref[...], x[:, :gd]) + _dot(s_ref[...], x[:, gd:])).astype(bf16)

    out = pl.pallas_call(
        kern,
        grid=(cfg.BATCH, cfg.F_GROUPS),
        in_specs=[pl.BlockSpec((L, L), lambda s, g: (0, 0)),
                  pl.BlockSpec((L, L), lambda s, g: (0, 0)),
                  pl.BlockSpec((1, L, 2 * gd), lambda s, g: (s, 0, g))],
        out_specs=pl.BlockSpec((1, L, gd), lambda s, g: (s, 0, g)),
        out_shape=jax.ShapeDtypeStruct((cfg.BATCH, L, D), bf16),
        compiler_params=_cparams(("arbitrary", "arbitrary")),
        name="prompt_dft",
    )(c_seq, s_seq, a3)
    return out.reshape(cfg.seg_len, D)


def _fft_stage1_call(a, k1m, tw_re, tw_im, cfg):
    D, R, J = cfg.D, cfg.R, cfg.FFT_J
    gd = D // cfg.F_GROUPS
    w = 2 * D
    rj = R * J
    a4 = a.reshape(cfg.n_seg, R, R, w)

    def kern(km_ref, twr_ref, twi_ref, x_ref, o_ref):
        pq = _dot(km_ref[...], x_ref[0].reshape(rj, 2 * gd))
        p, q = pq[:rj], pq[rj:]
        t_re = p[:, :gd] - q[:, gd:]
        t_im = q[:, :gd] + p[:, gd:]
        twr = jnp.tile(twr_ref[0], (1, gd // 128))
        twi = jnp.tile(twi_ref[0], (1, gd // 128))
        o_ref[0, :, :, :gd] = (t_re * twr - t_im * twi).astype(bf16).reshape(R, J, gd)
        o_ref[0, :, :, gd:] = (t_re * twi + t_im * twr).astype(bf16).reshape(R, J, gd)

    return pl.pallas_call(
        kern,
        grid=(cfg.DEC_BATCH, R // J, cfg.F_GROUPS),
        in_specs=[pl.BlockSpec((2 * rj, rj), lambda b, j, g: (0, 0)),
                  pl.BlockSpec((1, rj, 128), lambda b, j, g: (j, 0, 0)),
                  pl.BlockSpec((1, rj, 128), lambda b, j, g: (j, 0, 0)),
                  pl.BlockSpec((1, R, J, 2 * gd), lambda b, j, g: (b + 1, 0, j, g))],
        out_specs=pl.BlockSpec((1, R, J, 2 * gd), lambda b, j, g: (b, 0, j, g)),
        out_shape=jax.ShapeDtypeStruct((cfg.DEC_BATCH, R, R, w), bf16),
        compiler_params=_cparams(("arbitrary", "arbitrary", "arbitrary")),
        name="fft_stage1",
    )(k1m, tw_re, tw_im, a4)


def _fft_stage2_call(t4, k2c, k2s, cfg):
    D, R, KB = cfg.D, cfg.R, cfg.FFT_KB
    gd = D // cfg.F_GROUPS
    w = 2 * D
    t3 = t4.reshape(cfg.DEC_BATCH, R * R, w)

    def kern(kc_ref, ks_ref, x_ref, o_ref):
        x = x_ref[0]
        yv = _dot(kc_ref[...], x[:, :gd]) + _dot(ks_ref[...], x[:, gd:])
        o_ref[0] = yv.astype(bf16).reshape(R, KB, gd)

    out = pl.pallas_call(
        kern,
        grid=(cfg.DEC_BATCH, R // KB, cfg.F_GROUPS),
        in_specs=[pl.BlockSpec((R * KB, KB * R), lambda b, k, g: (0, 0)),
                  pl.BlockSpec((R * KB, KB * R), lambda b, k, g: (0, 0)),
                  pl.BlockSpec((1, KB * R, 2 * gd), lambda b, k, g: (b, k, g))],
        out_specs=pl.BlockSpec((1, R, KB, gd), lambda b, k, g: (b, 0, k, g)),
        out_shape=jax.ShapeDtypeStruct((cfg.DEC_BATCH, R, R, D), bf16),
        compiler_params=_cparams(("arbitrary", "arbitrary", "arbitrary")),
        name="fft_stage2",
    )(k2c, k2s, t3)
    return out.reshape(cfg.DEC_BATCH * cfg.seg_len, D)


def _post_call(fs, w_out, y, modr, layer, which, ln_g, ln_b, cfg):
    N, D, tm = cfg.N, cfg.D, cfg.TM
    spt = cfg.seg_len // tm
    nf, ny = len(fs), len(y)
    alpha = cfg.alpha

    def kern(*refs):
        f_refs, w_ref, y_refs = refs[:nf], refs[nf], refs[nf + 1:nf + 1 + ny]
        g_ref, lg_ref, lb_ref, o_ref = refs[nf + 1 + ny:]
        m = _dot(_rows_read(f_refs, spt), w_ref[...])
        z = alpha * _rows_read(y_refs, spt) + g_ref[0] * m
        o_ref[...] = _ln(z) * lg_ref[...] + lb_ref[...]

    return pl.pallas_call(
        kern,
        grid=(N // tm,),
        in_specs=_rows_specs(nf, tm, cfg, D) + [pl.BlockSpec((D, D), lambda i: (0, 0))]
        + _rows_specs(ny, tm, cfg, D) + [_mod_spec(cfg, layer, which, tm),
                                         pl.BlockSpec((1, D), lambda i: (0, 0)),
                                         pl.BlockSpec((1, D), lambda i: (0, 0))],
        out_specs=pl.BlockSpec((tm, D), lambda i: (i, 0)),
        out_shape=jax.ShapeDtypeStruct((N, D), f32),
        compiler_params=_cparams(("arbitrary",)),
        name="mixer_post",
    )(*fs, w_out, *y, modr, ln_g.reshape(1, D), ln_b.reshape(1, D))


def _mla_a_call(y, modr, layer, w_a2, g_q, g_kv, cos_t, sin_t, cfg):
    N, D, tm = cfg.N, cfg.D, cfg.TM
    QL, KVL, RP = cfg.QL, cfg.KVL, cfg.ROPE
    wa = w_a2.shape[1]
    o_kr = QL + KVL
    o_rot = o_kr + 128

    def rms(x, g):
        return x * lax.rsqrt(jnp.mean(x * x, axis=-1, keepdims=True) + RMS_EPS) * g

    def kern(y_ref, sh_ref, sc_ref, w_ref, gq_ref, gkv_ref, cos_ref, sin_ref,
             cq_ref, ckv_ref, kr_ref, krr_ref):
        u = _ln(y_ref[...]) * (1.0 + sc_ref[0]) + sh_ref[0]
        a = _dot(u.astype(bf16), w_ref[...])
        cq_ref[...] = rms(a[:, :QL], gq_ref[...]).astype(bf16)
        ckv_ref[...] = rms(a[:, QL:QL + KVL], gkv_ref[...])
        kr = a[:, o_kr:o_kr + RP]
        kr_ref[...] = kr
        krr_ref[...] = kr * cos_ref[...] + a[:, o_rot:o_rot + RP] * sin_ref[...]

    return pl.pallas_call(
        kern,
        grid=(N // tm,),
        in_specs=[pl.BlockSpec((tm, D), lambda i: (i, 0)),
                  _mod_spec(cfg, layer, 0, tm), _mod_spec(cfg, layer, 1, tm),
                  pl.BlockSpec((D, wa), lambda i: (0, 0)),
                  pl.BlockSpec((1, QL), lambda i: (0, 0)),
                  pl.BlockSpec((1, KVL), lambda i: (0, 0)),
                  pl.BlockSpec((tm, RP), lambda i: (i, 0)),
                  pl.BlockSpec((tm, RP), lambda i: (i, 0))],
        out_specs=[pl.BlockSpec((tm, QL), lambda i: (i, 0)),
                   pl.BlockSpec((tm, KVL), lambda i: (i, 0)),
                   pl.BlockSpec((tm, RP), lambda i: (i, 0)),
                   pl.BlockSpec((tm, RP), lambda i: (i, 0))],
        out_shape=[jax.ShapeDtypeStruct((N, QL), bf16),
                   jax.ShapeDtypeStruct((N, KVL), f32),
                   jax.ShapeDtypeStruct((N, RP), f32),
                   jax.ShapeDtypeStruct((N, RP), f32)],
        compiler_params=_cparams(("arbitrary",)),
        name="mla_down",
    )(y, modr, modr, w_a2, g_q.reshape(1, QL), g_kv.reshape(1, KVL), cos_t, sin_t)


def _q_call(cq, w_q, cos_t, sin_t, cfg):
    N, tm, H = cfg.N, cfg.TM, cfg.H
    QL, NP, RP = cfg.QL, cfg.NOPE, cfg.ROPE
    dk = NP + RP
    wq = NP + 2 * RP
    qscale = LOG2E / math.sqrt(dk)

    def kern(cq_ref, w_ref, cos_ref, sin_ref, q_ref):
        cq_t = cq_ref[...]
        cos, sin = cos_ref[...], sin_ref[...]
        for h in range(H):
            t = _dot(cq_t, w_ref[h])
            rope = t[:, NP:NP + RP] * cos + t[:, NP + RP:] * sin
            q_ref[h] = (jnp.concatenate([t[:, :NP], rope], axis=-1) * qscale).astype(bf16)

    return pl.pallas_call(
        kern,
        grid=(N // tm,),
        in_specs=[pl.BlockSpec((tm, QL), lambda i: (i, 0)),
                  pl.BlockSpec((H, QL, wq), lambda i: (0, 0, 0)),
                  pl.BlockSpec((tm, RP), lambda i: (i, 0)),
                  pl.BlockSpec((tm, RP), lambda i: (i, 0))],
        out_specs=pl.BlockSpec((H, tm, dk), lambda i: (0, i, 0)),
        out_shape=jax.ShapeDtypeStruct((H, N, dk), bf16),
        compiler_params=_cparams(("arbitrary",)),
        name="mla_q",
    )(cq, w_q, cos_t, sin_t)


def _kv_call(ckv_all, kr_all, w_kv, cfg):
    H, KVL, NP, RP, VD = cfg.H, cfg.KVL, cfg.NOPE, cfg.ROPE, cfg.VD
    LK, tkv, ns = cfg.LK, cfg.TKV, cfg.n_seg
    dk = NP + RP

    def kern(c_ref, kr_ref, w_ref, k_ref, v_ref):
        c = c_ref[0].astype(bf16)
        kr = kr_ref[0]
        for h in range(H):
            t = _dot(c, w_ref[h])
            k_ref[h, 0] = jnp.concatenate([t[:, :NP], kr], axis=-1).astype(bf16)
            v_ref[h, 0] = jnp.concatenate([t[:, NP:], jnp.ones((tkv, 128), f32)], axis=-1).astype(bf16)

    return pl.pallas_call(
        kern,
        grid=(ns, LK // tkv),
        in_specs=[pl.BlockSpec((1, tkv, KVL), lambda s, i: (s, i, 0)),
                  pl.BlockSpec((1, tkv, RP), lambda s, i: (s, i, 0)),
                  pl.BlockSpec((H, KVL, NP + VD), lambda s, i: (0, 0, 0))],
        out_specs=[pl.BlockSpec((H, 1, tkv, dk), lambda s, i: (0, s, i, 0)),
                   pl.BlockSpec((H, 1, tkv, VD + 128), lambda s, i: (0, s, i, 0))],
        out_shape=[jax.ShapeDtypeStruct((H, ns, LK, dk), bf16),
                   jax.ShapeDtypeStruct((H, ns, LK, VD + 128), bf16)],
        compiler_params=_cparams(("arbitrary", "arbitrary")),
        name="mla_kv",
    )(ckv_all, kr_all, w_kv)


def _attention_call(q, k, v, cfg):
    H, NP, RP, VD = cfg.H, cfg.NOPE, cfg.ROPE, cfg.VD
    assert VD == 128
    dk = NP + RP
    LK, TQ, SEQ, HB = cfg.LK, cfg.TQ, cfg.SEQ, cfg.HB
    nq = cfg.seg_len // TQ
    sub = TQ // SEQ

    def scores(qb, kb):
        return lax.dot_general(qb, kb, (((1,), (1,)), ((), ())), preferred_element_type=f32)

    def finish(s, vb):
        p = jnp.exp2(s - jnp.max(s, axis=-1, keepdims=True)).astype(bf16)
        acc = _dot(p, vb)
        return (acc[:, :VD] / acc[:, VD:]).astype(bf16)

    def kern(q_ref, k_ref, v_ref, o_ref):
        seg = pl.program_id(0)
        qi = pl.program_id(2)

        @pl.when(seg == 0)
        def _():
            for hh in range(HB):
                for j in range(sub):
                    off = pl.multiple_of((qi * sub + j) * SEQ, SEQ)
                    s = scores(q_ref[hh, j * SEQ:(j + 1) * SEQ, :], k_ref[hh, 0, pl.ds(off, SEQ), :])
                    o_ref[j * SEQ:(j + 1) * SEQ, hh * VD:(hh + 1) * VD] = finish(
                        s, v_ref[hh, 0, pl.ds(off, SEQ), :])

        @pl.when(seg != 0)
        def _():
            ss = [scores(q_ref[hh], k_ref[hh, 0]) for hh in range(HB)]
            for hh in range(HB):
                o_ref[:, hh * VD:(hh + 1) * VD] = finish(ss[hh], v_ref[hh, 0])

    return pl.pallas_call(
        kern,
        grid=(cfg.n_seg, H // HB, nq),
        in_specs=[pl.BlockSpec((HB, TQ, dk), lambda s, h, i: (h, s * nq + i, 0)),
                  pl.BlockSpec((HB, 1, LK, dk), lambda s, h, i: (h, s, 0, 0)),
                  pl.BlockSpec((HB, 1, LK, VD + 128), lambda s, h, i: (h, s, 0, 0))],
        out_specs=pl.BlockSpec((TQ, HB * VD), lambda s, h, i: (s * nq + i, h)),
        out_shape=jax.ShapeDtypeStruct((cfg.N, H * VD), bf16),
        compiler_params=_cparams(("arbitrary", "arbitrary", "arbitrary")),
        name="mla_attention",
    )(q, k, v)


def _router_call(y, modr, layer, wr2, b_router, cfg):
    N, D, tm, E, NG = cfg.N, cfg.D, cfg.TM, cfg.E, cfg.NG
    epg = E // NG

    def top2(x, jio):
        m1 = jnp.max(x, axis=0, keepdims=True)
        i1 = jnp.min(jnp.where(x == m1, jio, epg), axis=0, keepdims=True)
        x2 = jnp.where(jio == i1, -jnp.inf, x)
        m2 = jnp.max(x2, axis=0, keepdims=True)
        i2 = jnp.min(jnp.where(x2 == m2, jio, epg), axis=0, keepdims=True)
        return m1, i1, m2, i2

    def kern(y_ref, sh_ref, sc_ref, w_ref, b_ref, u_ref, e_ref, g_ref, cnt_ref):
        i = pl.program_id(0)
        u = _ln(y_ref[...]) * (1.0 + sc_ref[0]) + sh_ref[0]
        u_hi = u.astype(bf16)
        w_top = lax.bitcast_convert_type(u_hi[:, :D // 2].astype(f32), jnp.uint32)
        w_bot = lax.bitcast_convert_type(u_hi[:, D // 2:].astype(f32), jnp.uint32)
        u_ref[...] = w_top | (w_bot >> 16)
        u_lo = (u - u_hi.astype(f32)).astype(bf16)
        out = _dot(jnp.concatenate([u_hi, u_lo], axis=1), w_ref[...])
        lt = out.T
        logits = lt[:E] + lt[E:]
        scores = _sigmoid(logits)
        biased = scores + b_ref[...]
        jio = lax.broadcasted_iota(i32, (epg, tm), 0)
        best = gi = None
        for g in range(NG):
            m1, _, m2, _ = top2(biased[g * epg:(g + 1) * epg], jio)
            gs = m1 + m2
            if g == 0:
                best, gi = gs, jnp.zeros((1, tm), i32)
            else:
                better = gs > best
                best = jnp.where(better, gs, best)
                gi = jnp.where(better, g, gi)
        sel_b = biased[:epg]
        sel_s = scores[:epg]
        for g in range(1, NG):
            pick = gi == g
            sel_b = jnp.where(pick, biased[g * epg:(g + 1) * epg], sel_b)
            sel_s = jnp.where(pick, scores[g * epg:(g + 1) * epg], sel_s)
        _, i1, _, i2 = top2(sel_b, jio)
        s1 = jnp.sum(jnp.where(jio == i1, sel_s, 0.0), axis=0, keepdims=True)
        s2 = jnp.sum(jnp.where(jio == i2, sel_s, 0.0), axis=0, keepdims=True)
        den = s1 + s2
        e1 = gi * epg + i1
        e2 = gi * epg + i2
        e_ref[0:1, :] = e1
        e_ref[1:2, :] = e2
        rio = lax.broadcasted_iota(i32, (128, tm), 0)
        gates = jnp.where(rio == 0, ROUTED_SCALE * s1 / den,
                          jnp.where(rio == 1, ROUTED_SCALE * s2 / den, 0.0))
        g_ref[...] = gates.T
        eio = lax.broadcasted_iota(i32, (E, tm), 0)
        hits = jnp.where(eio == e1, 1.0, 0.0) + jnp.where(eio == e2, 1.0, 0.0)
        part = jnp.sum(hits, axis=1, keepdims=True)

        @pl.when(i == 0)
        def _():
            cnt_ref[...] = jnp.zeros_like(cnt_ref)

        cnt_ref[...] += part

    return pl.pallas_call(
        kern,
        grid=(N // tm,),
        in_specs=[pl.BlockSpec((tm, D), lambda i: (i, 0)),
                  _mod_spec(cfg, layer, 3, tm), _mod_spec(cfg, layer, 4, tm),
                  pl.BlockSpec((2 * D, 2 * E), lambda i: (0, 0)),
                  pl.BlockSpec((E, 1), lambda i: (0, 0))],
        out_specs=[pl.BlockSpec((tm, D // 2), lambda i: (i, 0)),
                   pl.BlockSpec((2, tm), lambda i: (0, i)),
                   pl.BlockSpec((tm, 128), lambda i: (i, 0)),
                   pl.BlockSpec((E, 128), lambda i: (0, 0))],
        out_shape=[jax.ShapeDtypeStruct((N, D // 2), jnp.uint32),
                   jax.ShapeDtypeStruct((2, N), i32),
                   jax.ShapeDtypeStruct((N, 128), f32),
                   jax.ShapeDtypeStruct((E, 128), f32)],
        compiler_params=_cparams(("arbitrary",)),
        name="moe_router",
    )(y, modr, modr, wr2, b_router.reshape(E, 1))


def _rank_call(e_flat, pstart, tri, cfg):
    E, C = cfg.E, cfg.RC
    n_asg = e_flat.shape[1]

    def kern(e_ref, ps_ref, tri_ref, d_ref, base_ref):
        @pl.when(pl.program_id(0) == 0)
        def _():
            base_ref[...] = jnp.zeros_like(base_ref)

        eio = lax.broadcasted_iota(i32, (E, C), 0)
        oh = eio == e_ref[...]
        incl = _dot(jnp.where(oh, 1.0, 0.0).astype(bf16), tri_ref[...])
        base = base_ref[:, 0:1]
        pos = ps_ref[...] + base + incl - 1.0
        d_ref[...] = jnp.sum(jnp.where(oh, pos, 0.0), axis=0, keepdims=True).astype(i32)
        base_ref[...] += incl[:, C - 1:C]

    return pl.pallas_call(
        kern,
        grid=(n_asg // C,),
        in_specs=[pl.BlockSpec((1, C), lambda i: (0, i)),
                  pl.BlockSpec((E, 1), lambda i: (0, 0)),
                  pl.BlockSpec((C, C), lambda i: (0, 0))],
        out_specs=pl.BlockSpec((1, C), lambda i: (0, i)),
        out_shape=jax.ShapeDtypeStruct((1, n_asg), i32),
        scratch_shapes=[pltpu.VMEM((E, 128), f32)],
        compiler_params=_cparams(("arbitrary",)),
        name="moe_rank",
    )(e_flat, pstart, tri)


def _dispatch_call(dest3, u, xs0, cfg):
    N, D, tm = cfg.N, cfg.D, cfg.TM

    def kern(d_ref, u_ref, xs_in_ref, xs_ref, sem):
        del xs_in_ref

        def issue(r, c):
            for k in range(2):
                pltpu.make_async_copy(u_ref.at[pl.ds(r, 1), :],
                                      xs_ref.at[pl.ds(d_ref[0, k, r], 1), :], sem).start()
            return c

        lax.fori_loop(0, tm, issue, 0, unroll=8)
        for k in range(2):
            pltpu.make_async_copy(u_ref, xs_ref.at[pl.ds(0, tm), :], sem).wait()

    return pl.pallas_call(
        kern,
        grid=(N // tm,),
        in_specs=[pl.BlockSpec((1, 2, tm), lambda i: (i, 0, 0), memory_space=pltpu.SMEM),
                  pl.BlockSpec((tm, D // 2), lambda i: (i, 0)),
                  pl.BlockSpec(memory_space=pl.ANY)],
        out_specs=pl.BlockSpec(memory_space=pl.ANY),
        out_shape=jax.ShapeDtypeStruct(xs0.shape, xs0.dtype),
        scratch_shapes=[pltpu.SemaphoreType.DMA(())],
        input_output_aliases={2: 0},
        compiler_params=_cparams(("arbitrary",)),
        name="moe_dispatch",
    )(dest3, u, xs0)


def _expert_call(tile_e, tile_src, tile_used, tile_next, tile_slot, xs, w_exp_in, w_exp_out, layer, cfg):
    D, DE, TE = cfg.D, cfg.DE, cfg.TE
    n_rows = xs.shape[0]
    n_tiles = n_rows // TE

    def kern(te_ref, ts_ref, tu_ref, tn_ref, tl_ref, x_ref, wi_hbm, wo_hbm, y_ref,
             wi_f, wo_f, wi_bf, wo_bf, sem):
        i = pl.program_id(0)
        e = te_ref[i]
        slot = tl_ref[i]
        first = (i == 0) | (e != te_ref[jnp.maximum(i - 1, 0)])

        def copies(expert, s):
            return (pltpu.make_async_copy(wi_hbm.at[layer, expert], wi_f.at[s], sem.at[0, s]),
                    pltpu.make_async_copy(wo_hbm.at[layer, expert], wo_f.at[s], sem.at[1, s]))

        @pl.when(i == 0)
        def _():
            for cp in copies(e, slot):
                cp.start()

        @pl.when(first)
        def _():
            for cp in copies(e, slot):
                cp.wait()
            nxt = tn_ref[i]

            @pl.when(nxt >= 0)
            def _():
                for cp in copies(nxt, 1 - slot):
                    cp.start()

            wi_bf[...] = wi_f[slot].astype(bf16)
            wo_bf[...] = wo_f[slot].astype(bf16)

        @pl.when(tu_ref[i] == 1)
        def _():
            w = x_ref[...]
            top = lax.bitcast_convert_type(w & jnp.uint32(0xFFFF0000), f32).astype(bf16)
            bot = lax.bitcast_convert_type(w << 16, f32).astype(bf16)
            h = _dot(jnp.concatenate([top, bot], axis=1), wi_bf[...])
            act = h[:, :DE] * _sigmoid(h[:, :DE]) * h[:, DE:]
            y_ref[...] = _dot(act.astype(bf16), wo_bf[...])

        @pl.when(tu_ref[i] == 0)
        def _():
            y_ref[...] = jnp.zeros_like(y_ref)

    grid_spec = pltpu.PrefetchScalarGridSpec(
        num_scalar_prefetch=5,
        grid=(n_tiles,),
        in_specs=[pl.BlockSpec((TE, D // 2), lambda i, te, ts, tu, tn, tl: (ts[i], 0)),
                  pl.BlockSpec(memory_space=pl.ANY),
                  pl.BlockSpec(memory_space=pl.ANY)],
        out_specs=pl.BlockSpec((TE, D), lambda i, te, ts, tu, tn, tl: (i, 0)),
        scratch_shapes=[pltpu.VMEM((2, D, 2 * DE), f32), pltpu.VMEM((2, DE, D), f32),
                        pltpu.VMEM((D, 2 * DE), bf16), pltpu.VMEM((DE, D), bf16),
                        pltpu.SemaphoreType.DMA((2, 2))],
    )
    return pl.pallas_call(
        kern,
        grid_spec=grid_spec,
        out_shape=jax.ShapeDtypeStruct((n_rows, D), f32),
        compiler_params=_cparams(("arbitrary",)),
        name="moe_experts",
    )(tile_e, tile_src, tile_used, tile_next, tile_slot, xs, w_exp_in, w_exp_out)


def _combine_call(dest3, y, gt, ys, modr, layer, ln_g, ln_b, cfg, split_out=False):
    N, D, tm = cfg.N, cfg.D, cfg.TM
    alpha = cfg.alpha
    spt = cfg.seg_len // tm
    n_out = 2 if split_out else 1

    def kern(d_ref, y_ref, gt_ref, g2_ref, lg_ref, lb_ref, ys_ref, *rest):
        o_refs, (r0, r1, sem) = rest[:n_out], rest[n_out:]
        bufs = (r0, r1)

        def issue(r, c):
            for k in range(2):
                pltpu.make_async_copy(ys_ref.at[pl.ds(d_ref[0, k, r], 1), :],
                                      bufs[k].at[pl.ds(r, 1), :], sem).start()
            return c

        lax.fori_loop(0, tm, issue, 0, unroll=8)
        for k in range(2):
            pltpu.make_async_copy(ys_ref.at[pl.ds(0, tm), :], bufs[k], sem).wait()
        gt_t = gt_ref[...]
        f = gt_t[:, 0:1] * r0[...] + gt_t[:, 1:2] * r1[...]
        z = alpha * y_ref[...] + g2_ref[0] * f
        res = _ln(z) * lg_ref[...] + lb_ref[...]
        if split_out:
            @pl.when(pl.program_id(0) < spt)
            def _():
                o_refs[0][...] = res

            @pl.when(pl.program_id(0) >= spt)
            def _():
                o_refs[1][...] = res
        else:
            o_refs[0][...] = res

    if split_out:
        out_specs = _rows_specs(2, tm, cfg, D)
        out_shape = [jax.ShapeDtypeStruct((cfg.seg_len, D), f32),
                     jax.ShapeDtypeStruct((N - cfg.seg_len, D), f32)]
    else:
        out_specs = pl.BlockSpec((tm, D), lambda i: (i, 0))
        out_shape = jax.ShapeDtypeStruct((N, D), f32)
    return pl.pallas_call(
        kern,
        grid=(N // tm,),
        in_specs=[pl.BlockSpec((1, 2, tm), lambda i: (i, 0, 0), memory_space=pltpu.SMEM),
                  pl.BlockSpec((tm, D), lambda i: (i, 0)),
                  pl.BlockSpec((tm, 128), lambda i: (i, 0)),
                  _mod_spec(cfg, layer, 5, tm),
                  pl.BlockSpec((1, D), lambda i: (0, 0)),
                  pl.BlockSpec((1, D), lambda i: (0, 0)),
                  pl.BlockSpec(memory_space=pl.ANY)],
        out_specs=out_specs,
        out_shape=out_shape,
        scratch_shapes=[pltpu.VMEM((tm, D), f32), pltpu.VMEM((tm, D), f32),
                        pltpu.SemaphoreType.DMA(())],
        compiler_params=_cparams(("arbitrary",)),
        name="moe_combine",
    )(dest3, y, gt, modr, ln_g.reshape(1, D), ln_b.reshape(1, D), ys)


def _moe_layer(y, xs_buf, modr, layer, wr2, b_router, tri, w_exp_in, w_exp_out, ln_g, ln_b, cfg, split_out):
    N, D, E, TE, tm = cfg.N, cfg.D, cfg.E, cfg.TE, cfg.TM
    u, e2, gt, cnt = _router_call(y, modr, layer, wr2, b_router, cfg)
    counts = cnt[:, 0].astype(i32)
    padded = (counts + TE - 1) // TE * TE
    pend = jnp.cumsum(padded)
    pstart = pend - padded
    n_rows = 2 * N + E * TE
    n_tiles = n_rows // TE
    tile_start = jnp.arange(n_tiles, dtype=i32) * TE
    n_used = pend[-1] // TE
    tile_src = jnp.minimum(jnp.arange(n_tiles, dtype=i32), n_used - 1)
    tile_e = jnp.minimum(jnp.sum((pend[None, :] <= (tile_src * TE)[:, None]).astype(i32), axis=1), E - 1)
    tile_used = (tile_start < pend[-1]).astype(i32)
    eids = jnp.arange(E, dtype=i32)
    nonempty = counts > 0
    later = (eids[None, :] > eids[:, None]) & nonempty[None, :]
    next_e = jnp.min(jnp.where(later, eids[None, :], E), axis=1)
    next_e = jnp.where(next_e >= E, -1, next_e)
    run_idx = jnp.cumsum(nonempty.astype(i32)) - 1
    tile_next = next_e[tile_e]
    tile_slot = run_idx[tile_e] % 2

    dest = _rank_call(e2.reshape(1, 2 * N), pstart.astype(f32).reshape(E, 1), tri, cfg)
    dest3 = dest.reshape(2, N // tm, tm).transpose(1, 0, 2)
    xs = _dispatch_call(dest3, u, xs_buf, cfg)
    ys = _expert_call(tile_e, tile_src, tile_used, tile_next, tile_slot, xs, w_exp_in, w_exp_out, layer, cfg)
    return _combine_call(dest3, y, gt, ys, modr, layer, ln_g, ln_b, cfg, split_out), xs


def _dft_tables(cfg):
    D, R, L = cfg.D, cfg.R, cfg.SEQ
    gd = D // cfg.F_GROUPS

    def cs(n, scale):
        k = np.arange(n)
        ang = 2.0 * np.pi * ((k[:, None] * k[None, :]) % n) / n
        return np.cos(ang) * scale, np.sin(ang) * scale

    cc, sc = cs(gd, 1.0 / math.sqrt(gd))
    chan = jnp.asarray(np.concatenate([cc, -sc], axis=1), bf16)
    cl, sl = cs(L, 1.0 / math.sqrt(L))
    cr, sr = cs(R, 1.0 / math.sqrt(R))
    J, KB = cfg.FFT_J, cfg.FFT_KB
    f2 = np.concatenate([cr, -sr], axis=0)
    k1m = jnp.asarray(np.kron(f2, np.eye(J)), bf16)
    eye = np.eye(KB)
    k2c = jnp.asarray(np.einsum("kn,ab->kabn", cr, eye).reshape(R * KB, KB * R), bf16)
    k2s = jnp.asarray(np.einsum("kn,ab->kabn", sr, eye).reshape(R * KB, KB * R), bf16)
    k1 = np.arange(R)
    ang = 2.0 * np.pi * (k1[:, None] * k1[None, :]) / (R * R)
    ang = ang.reshape(R, R // J, J).transpose(1, 0, 2).reshape(R // J, R * J)
    tw_re = jnp.asarray(np.repeat(np.cos(ang)[:, :, None], 128, axis=2), f32)
    tw_im = jnp.asarray(np.repeat(-np.sin(ang)[:, :, None], 128, axis=2), f32)
    return dict(chan=chan, c_seq=jnp.asarray(cl, bf16), s_seq=jnp.asarray(sl, bf16), k1m=k1m,
                k2c=k2c, k2s=k2s, tw_re=tw_re, tw_im=tw_im)


def _rot_cols(w):
    shp = w.shape
    w2 = w.reshape(shp[:-1] + (shp[-1] // 2, 2))
    return jnp.stack([-w2[..., 1], w2[..., 0]], axis=-1).reshape(shp)


def _rope_tables(cfg):
    L, RP = cfg.seg_len, cfg.ROPE
    pairs = RP // 4
    rows = L // cfg.GRID_W
    t_row = jnp.repeat(jnp.arange(rows, dtype=f32), cfg.GRID_W)
    t_col = jnp.tile(jnp.arange(cfg.GRID_W, dtype=f32), rows)
    inv = ROPE_BASE ** (-jnp.arange(pairs, dtype=f32) / pairs)
    ang = jnp.concatenate([t_row[:, None] * inv, t_col[:, None] * inv], -1)
    cos = jnp.repeat(jnp.cos(ang), 2, axis=-1)
    sin = jnp.repeat(jnp.sin(ang), 2, axis=-1)
    cos_t = jnp.concatenate([jnp.ones((L, RP), f32), jnp.tile(cos, (cfg.DEC_BATCH, 1))], 0)
    sin_t = jnp.concatenate([jnp.zeros((L, RP), f32), jnp.tile(sin, (cfg.DEC_BATCH, 1))], 0)
    return cos_t, sin_t


def _mla_weights(w_a, w_uq, w_ukv, cfg):
    D, H, QL, KVL, NP, RP, VD = cfg.D, cfg.H, cfg.QL, cfg.KVL, cfg.NOPE, cfg.ROPE, cfg.VD
    kr_w = w_a[:, QL + KVL:]
    z = jnp.zeros((D, 128 - RP), f32)
    w_a2 = jnp.concatenate([w_a[:, :QL + KVL], kr_w, z, _rot_cols(kr_w), z], axis=1).astype(bf16)
    wq = w_uq.reshape(QL, H, NP + RP)
    w_q = jnp.concatenate([wq, _rot_cols(wq[..., NP:])], axis=-1).transpose(1, 0, 2).astype(bf16)
    w_kv = w_ukv.reshape(KVL, H, NP + VD).transpose(1, 0, 2).astype(bf16)
    return w_a2, w_q, w_kv


def _forward(cfg, x_prompt, x_sample, c, cache_ckv, cache_krope, c_ctx, w_ada, b_ada, w_f_in, w_f_out,
             w_mla_a, g_mla_q, g_mla_kv, w_mla_uq, w_mla_ukv, w_mla_o, ln_mix_g, ln_mix_b,
             ln_ffn_g, ln_ffn_b, w_router, b_router, w_exp_in, w_exp_out):
    assert cfg.BATCH * cfg.SEQ == cfg.seg_len and cfg.R * cfg.R == cfg.seg_len
    assert cfg.TQ % cfg.SEQ == 0 and cfg.LK % cfg.TKV == 0
    assert cfg.R % cfg.FFT_J == 0 and cfg.R % cfg.FFT_KB == 0
    N, D, L, E = cfg.N, cfg.D, cfg.seg_len, cfg.E
    assert cfg.DEPTH >= 2
    y = (x_prompt.reshape(L, D), x_sample.reshape(cfg.DEC_BATCH * L, D))

    cvec = jnp.concatenate([c_ctx[None], c, jnp.zeros((MOD_ROWS - cfg.n_seg, D), f32)], axis=0)
    mod = _ada_call(cvec, w_ada, b_ada, cfg)
    modr = mod.reshape(cfg.DEPTH * MOD_ROWS * 6, 1, D)

    tabs = _dft_tables(cfg)
    cos_t, sin_t = _rope_tables(cfg)
    w_hi = w_router.astype(bf16)
    w_lo = (w_router - w_hi.astype(f32)).astype(bf16)
    wr2 = jnp.concatenate([jnp.concatenate([w_hi, w_lo], 1),
                           jnp.concatenate([w_hi, jnp.zeros_like(w_lo)], 1)], 0)
    rc = cfg.RC
    tri = jnp.asarray(np.triu(np.ones((rc, rc), np.float32)), bf16)

    ckv_out, kr_out = [], []
    xs_buf = jnp.zeros((2 * N + E * cfg.TE, D // 2), jnp.uint32)
    for l in range(cfg.DEPTH):
        j = l // 2
        if l % 2 == 0:
            yt = y if isinstance(y, tuple) else (y,)
            a = _fourier_in_call(yt, modr, l, w_f_in[j].astype(bf16), tabs["chan"], cfg)
            f_p = _prompt_dft_call(a, tabs["c_seq"], tabs["s_seq"], cfg)
            t = _fft_stage1_call(a, tabs["k1m"], tabs["tw_re"], tabs["tw_im"], cfg)
            f_s = _fft_stage2_call(t, tabs["k2c"], tabs["k2s"], cfg)
            y = _post_call((f_p, f_s), w_f_out[j].astype(bf16), yt, modr, l, 2, ln_mix_g[l], ln_mix_b[l], cfg)
        else:
            w_a2, w_q, w_kv = _mla_weights(w_mla_a[j], w_mla_uq[j], w_mla_ukv[j], cfg)
            cq, ckv, kr, krr = _mla_a_call(y, modr, l, w_a2, g_mla_q[j], g_mla_kv[j], cos_t, sin_t, cfg)
            ckv_out.append(ckv[:L].reshape(cfg.BATCH, cfg.SEQ, cfg.KVL))
            kr_out.append(kr[:L].reshape(cfg.BATCH, cfg.SEQ, cfg.ROPE))
            q = _q_call(cq, w_q, cos_t, sin_t, cfg)
            pad = cfg.PAST
            ckv_all = jnp.concatenate([
                jnp.concatenate([ckv[:L], jnp.zeros((pad, cfg.KVL), f32)], 0)[None],
                jnp.concatenate([cache_ckv[:, j], ckv[L:].reshape(cfg.DEC_BATCH, L, cfg.KVL)], 1)], 0)
            kr_all = jnp.concatenate([
                jnp.concatenate([krr[:L], jnp.zeros((pad, cfg.ROPE), f32)], 0)[None],
                jnp.concatenate([cache_krope[:, j], krr[L:].reshape(cfg.DEC_BATCH, L, cfg.ROPE)], 1)], 0)
            k, v = _kv_call(ckv_all, kr_all, w_kv, cfg)
            o = _attention_call(q, k, v, cfg)
            y = _post_call((o,), w_mla_o[j].astype(bf16), (y,), modr, l, 2, ln_mix_g[l], ln_mix_b[l], cfg)
        y, xs_buf = _moe_layer(y, xs_buf, modr, l, wr2, b_router, tri, w_exp_in, w_exp_out, ln_ffn_g[l],
                               ln_ffn_b[l], cfg, split_out=(l == cfg.DEPTH - 1))

    y_p = y[0].reshape(cfg.BATCH, cfg.SEQ, D)
    y_s = y[1].reshape(cfg.DEC_BATCH, L, D)
    return y_p, y_s, jnp.stack(ckv_out, axis=1), jnp.stack(kr_out, axis=1)


def kernel(x_prompt, x_sample, c, cache_ckv, cache_krope, c_ctx, w_ada, b_ada, w_f_in, w_f_out, w_mla_a, g_mla_q, g_mla_kv, w_mla_uq, w_mla_ukv, w_mla_o, ln_mix_g, ln_mix_b, ln_ffn_g, ln_ffn_b, w_router, b_router, w_exp_in, w_exp_out):
    return _forward(Cfg(), x_prompt, x_sample, c, cache_ckv, cache_krope, c_ctx, w_ada, b_ada, w_f_in,
                    w_f_out, w_mla_a, g_mla_q, g_mla_kv, w_mla_uq, w_mla_ukv, w_mla_o, ln_mix_g,
                    ln_mix_b, ln_ffn_g, ln_ffn_b, w_router, b_router, w_exp_in, w_exp_out)
```

```python
import functools
import math
from typing import NamedTuple

import numpy as np
import jax
import jax.numpy as jnp
from jax import lax
from jax.experimental import pallas as pl
from jax.experimental.pallas import tpu as pltpu

f32 = jnp.float32
bf16 = jnp.bfloat16
i32 = jnp.int32

LN_EPS = 1e-5
RMS_EPS = 1e-6
ROPE_BASE = 10000.0
ROUTED_SCALE = 2.5
LOG2E = 1.4426950408889634
NEG_BIG = -1e30
MOD_ROWS = 16
VMEM_LIMIT_V7X = 56 * 1024 * 1024


class Cfg(NamedTuple):
    D: int = 2048
    DEPTH: int = 4
    BATCH: int = 16
    SEQ: int = 256
    DEC_BATCH: int = 8
    DEC_SEQ: int = 4096
    PAST: int = 256
    GRID_W: int = 64
    F_GROUPS: int = 4
    H: int = 16
    QL: int = 512
    KVL: int = 512
    NOPE: int = 128
    ROPE: int = 64
    VD: int = 128
    E: int = 64
    NG: int = 8
    DE: int = 512
    TM: int = 512
    TN_ADA: int = 1024
    FFT_J: int = 8
    FFT_KB: int = 8
    TKV: int = 256
    TQ: int = 512
    HB: int = 2
    RC: int = 1024
    TE: int = 256

    @property
    def seg_len(self):
        return self.DEC_SEQ

    @property
    def n_seg(self):
        return 1 + self.DEC_BATCH

    @property
    def N(self):
        return self.n_seg * self.seg_len

    @property
    def R(self):
        return int(round(math.sqrt(self.seg_len)))

    @property
    def LK(self):
        return self.PAST + self.seg_len

    @property
    def alpha(self):
        return (2 * self.DEPTH) ** 0.25


def _cparams(sem):
    return pltpu.CompilerParams(dimension_semantics=sem, vmem_limit_bytes=VMEM_LIMIT_V7X)


def _ln(x):
    mu = jnp.mean(x, axis=-1, keepdims=True)
    xc = x - mu
    var = jnp.mean(xc * xc, axis=-1, keepdims=True)
    return xc * lax.rsqrt(var + LN_EPS)


def _sigmoid(x):
    return 1.0 / (1.0 + jnp.exp(-x))


def _dot(a, b):
    return jnp.dot(a, b, preferred_element_type=f32)


def _pack2(hi, lo):
    h = lax.bitcast_convert_type(hi.astype(bf16).astype(f32), jnp.uint32)
    l = lax.bitcast_convert_type(lo.astype(bf16).astype(f32), jnp.uint32)
    return h | (l >> 16)


def _unpack2(w):
    hi = lax.bitcast_convert_type(w & jnp.uint32(0xFFFF0000), f32).astype(bf16)
    lo = lax.bitcast_convert_type(w << 16, f32).astype(bf16)
    return hi, lo


def _mod_spec(cfg, layer, which, tm):
    spt = cfg.seg_len // tm
    base = layer * MOD_ROWS * 6
    return pl.BlockSpec((1, 1, cfg.D), lambda i: (base + (i // spt) * 6 + which, 0, 0))


def _rows_specs(n_arrs, tm, cfg, width):
    spt = cfg.seg_len // tm
    if n_arrs == 2:
        return [pl.BlockSpec((tm, width), lambda i: (jnp.minimum(i, spt - 1), 0)),
                pl.BlockSpec((tm, width), lambda i: (jnp.maximum(i - spt, 0), 0))]
    return [pl.BlockSpec((tm, width), lambda i: (i, 0))]


def _rows_read(refs, spt):
    if len(refs) == 2:
        return jnp.where(pl.program_id(0) < spt, refs[0][...], refs[1][...])
    return refs[0][...]


def _ada_call(cvec, w_ada, b_ada, cfg):
    D, n6, tn = cfg.D, 6 * cfg.D, cfg.TN_ADA

    def kern(c_ref, w_ref, b_ref, o_ref):
        c = c_ref[...]
        s = (c * _sigmoid(c)).astype(bf16)
        o_ref[0] = _dot(s, w_ref[0].astype(bf16)) + b_ref[0]

    return pl.pallas_call(
        kern,
        grid=(cfg.DEPTH, n6 // tn),
        in_specs=[pl.BlockSpec((MOD_ROWS, D), lambda l, j: (0, 0)),
                  pl.BlockSpec((1, D, tn), lambda l, j: (l, 0, j)),
                  pl.BlockSpec((1, 1, tn), lambda l, j: (l, 0, j))],
        out_specs=pl.BlockSpec((1, MOD_ROWS, tn), lambda l, j: (l, 0, j)),
        out_shape=jax.ShapeDtypeStruct((cfg.DEPTH, MOD_ROWS, n6), f32),
        compiler_params=_cparams(("arbitrary", "arbitrary")),
        name="ada",
    )(cvec, w_ada, b_ada.reshape(cfg.DEPTH, 1, n6))


def _fourier_in_call(y, modr, layer, w_in, cs, cfg):
    N, D, tm = cfg.N, cfg.D, cfg.TM
    gd = D // cfg.F_GROUPS
    ny = len(y)
    spt = cfg.seg_len // tm

    def kern(*refs):
        y_refs, (sh_ref, sc_ref, w_ref, cs_ref, a_ref) = refs[:ny], refs[ny:]
        u = _ln(_rows_read(y_refs, spt)) * (1.0 + sc_ref[0]) + sh_ref[0]
        h = _dot(u.astype(bf16), w_ref[...]).astype(bf16)
        for g in range(cfg.F_GROUPS):
            a = _dot(h[:, g * gd:(g + 1) * gd], cs_ref[...])
            a_ref[:, g * gd:(g + 1) * gd] = _pack2(a[:, :gd], a[:, gd:])

    return pl.pallas_call(
        kern,
        grid=(N // tm,),
        in_specs=_rows_specs(ny, tm, cfg, D) + [
            _mod_spec(cfg, layer, 0, tm), _mod_spec(cfg, layer, 1, tm),
            pl.BlockSpec((D, D), lambda i: (0, 0)),
            pl.BlockSpec((gd, 2 * gd), lambda i: (0, 0))],
        out_specs=pl.BlockSpec((tm, D), lambda i: (i, 0)),
        out_shape=jax.ShapeDtypeStruct((N, D), jnp.uint32),
        compiler_params=_cparams(("arbitrary",)),
        name="fourier_in",
    )(*y, modr, modr, w_in, cs)


def _prompt_dft_call(a, c_seq, s_seq, cfg):
    D, L = cfg.D, cfg.SEQ
    gd = D // cfg.F_GROUPS
    a3 = a.reshape(cfg.N // L, L, D)

    def kern(c_ref, s_ref, x_ref, o_ref):
        re, im = _unpack2(x_ref[0])
        o_ref[0] = _dot(c_ref[...], re) + _dot(s_ref[...], im)

    out = pl.pallas_call(
        kern,
        grid=(cfg.BATCH, cfg.F_GROUPS),
        in_specs=[pl.BlockSpec((L, L), lambda s, g: (0, 0)),
                  pl.BlockSpec((L, L), lambda s, g: (0, 0)),
                  pl.BlockSpec((1, L, gd), lambda s, g: (s, 0, g))],
        out_specs=pl.BlockSpec((1, L, gd), lambda s, g: (s, 0, g)),
        out_shape=jax.ShapeDtypeStruct((cfg.BATCH, L, D), f32),
        compiler_params=_cparams(("arbitrary", "arbitrary")),
        name="prompt_dft",
    )(c_seq, s_seq, a3)
    return out.reshape(cfg.seg_len, D)


def _fft_stage1_call(a, k1m, tw_re, tw_im, cfg):
    D, R, J = cfg.D, cfg.R, cfg.FFT_J
    gd = D // cfg.F_GROUPS
    rj = R * J
    a4 = a.reshape(cfg.n_seg, R, R, D)

    def kern(km_ref, twr_ref, twi_ref, x_ref, o_ref):
        x = jnp.concatenate(_unpack2(x_ref[0].reshape(rj, gd)), axis=1)
        pq = _dot(km_ref[...], x)
        p, q = pq[:rj], pq[rj:]
        t_re = p[:, :gd] - q[:, gd:]
        t_im = q[:, :gd] + p[:, gd:]
        twr = jnp.tile(twr_ref[0], (1, gd // 128))
        twi = jnp.tile(twi_ref[0], (1, gd // 128))
        o_ref[0] = _pack2(t_re * twr - t_im * twi, t_re * twi + t_im * twr).reshape(R, J, gd)

    return pl.pallas_call(
        kern,
        grid=(cfg.DEC_BATCH, R // J, cfg.F_GROUPS),
        in_specs=[pl.BlockSpec((2 * rj, rj), lambda b, j, g: (0, 0)),
                  pl.BlockSpec((1, rj, 128), lambda b, j, g: (j, 0, 0)),
                  pl.BlockSpec((1, rj, 128), lambda b, j, g: (j, 0, 0)),
                  pl.BlockSpec((1, R, J, gd), lambda b, j, g: (b + 1, 0, j, g))],
        out_specs=pl.BlockSpec((1, R, J, gd), lambda b, j, g: (b, 0, j, g)),
        out_shape=jax.ShapeDtypeStruct((cfg.DEC_BATCH, R, R, D), jnp.uint32),
        compiler_params=_cparams(("arbitrary", "arbitrary", "arbitrary")),
        name="fft_stage1",
    )(k1m, tw_re, tw_im, a4)


def _fft_stage2_call(t4, k2c, k2s, cfg):
    D, R, KB = cfg.D, cfg.R, cfg.FFT_KB
    gd = D // cfg.F_GROUPS
    t3 = t4.reshape(cfg.DEC_BATCH, R * R, D)

    def kern(kc_ref, ks_ref, x_ref, o_ref):
        re, im = _unpack2(x_ref[0])
        yv = _dot(kc_ref[...], re) + _dot(ks_ref[...], im)
        o_ref[0] = yv.reshape(R, KB, gd)

    out = pl.pallas_call(
        kern,
        grid=(cfg.DEC_BATCH, R // KB, cfg.F_GROUPS),
        in_specs=[pl.BlockSpec((R * KB, KB * R), lambda b, k, g: (0, 0)),
                  pl.BlockSpec((R * KB, KB * R), lambda b, k, g: (0, 0)),
                  pl.BlockSpec((1, KB * R, gd), lambda b, k, g: (b, k, g))],
        out_specs=pl.BlockSpec((1, R, KB, gd), lambda b, k, g: (b, 0, k, g)),
        out_shape=jax.ShapeDtypeStruct((cfg.DEC_BATCH, R, R, D), f32),
        compiler_params=_cparams(("arbitrary", "arbitrary", "arbitrary")),
        name="fft_stage2",
    )(k2c, k2s, t3)
    return out.reshape(cfg.DEC_BATCH * cfg.seg_len, D)


def _post_call(fs, w_out, y, modr, layer, which, ln_g, ln_b, cfg):
    N, D, tm = cfg.N, cfg.D, cfg.TM
    spt = cfg.seg_len // tm
    nf, ny = len(fs), len(y)
    alpha = cfg.alpha

    def kern(*refs):
        f_refs, w_ref, y_refs = refs[:nf], refs[nf], refs[nf + 1:nf + 1 + ny]
        g_ref, lg_ref, lb_ref, o_ref = refs[nf + 1 + ny:]
        m = _dot(_rows_read(f_refs, spt).astype(bf16), w_ref[...])
        z = alpha * _rows_read(y_refs, spt) + g_ref[0] * m
        o_ref[...] = _ln(z) * lg_ref[...] + lb_ref[...]

    return pl.pallas_call(
        kern,
        grid=(N // tm,),
        in_specs=_rows_specs(nf, tm, cfg, D) + [pl.BlockSpec((D, D), lambda i: (0, 0))]
        + _rows_specs(ny, tm, cfg, D) + [_mod_spec(cfg, layer, which, tm),
                                         pl.BlockSpec((1, D), lambda i: (0, 0)),
                                         pl.BlockSpec((1, D), lambda i: (0, 0))],
        out_specs=pl.BlockSpec((tm, D), lambda i: (i, 0)),
        out_shape=jax.ShapeDtypeStruct((N, D), f32),
        compiler_params=_cparams(("arbitrary",)),
        name="mixer_post",
    )(*fs, w_out, *y, modr, ln_g.reshape(1, D), ln_b.reshape(1, D))


def _mla_a_call(y, modr, layer, w_a2, g_q, g_kv, cos_t, sin_t, cfg):
    N, D, tm = cfg.N, cfg.D, cfg.TM
    QL, KVL, RP = cfg.QL, cfg.KVL, cfg.ROPE
    wa = w_a2.shape[1]
    o_kr = QL + KVL
    o_rot = o_kr + 128

    def rms(x, g):
        return x * lax.rsqrt(jnp.mean(x * x, axis=-1, keepdims=True) + RMS_EPS) * g

    def kern(y_ref, sh_ref, sc_ref, w_ref, gq_ref, gkv_ref, cos_ref, sin_ref,
             cq_ref, ckv_ref, kr_ref, krr_ref):
        u = _ln(y_ref[...]) * (1.0 + sc_ref[0]) + sh_ref[0]
        a = _dot(u.astype(bf16), w_ref[...])
        cq_ref[...] = rms(a[:, :QL], gq_ref[...]).astype(bf16)
        ckv_ref[...] = rms(a[:, QL:QL + KVL], gkv_ref[...])
        kr = a[:, o_kr:o_kr + RP]
        kr_ref[...] = kr
        krr_ref[...] = kr * cos_ref[...] + a[:, o_rot:o_rot + RP] * sin_ref[...]

    return pl.pallas_call(
        kern,
        grid=(N // tm,),
        in_specs=[pl.BlockSpec((tm, D), lambda i: (i, 0)),
                  _mod_spec(cfg, layer, 0, tm), _mod_spec(cfg, layer, 1, tm),
                  pl.BlockSpec((D, wa), lambda i: (0, 0)),
                  pl.BlockSpec((1, QL), lambda i: (0, 0)),
                  pl.BlockSpec((1, KVL), lambda i: (0, 0)),
                  pl.BlockSpec((tm, RP), lambda i: (i, 0)),
                  pl.BlockSpec((tm, RP), lambda i: (i, 0))],
        out_specs=[pl.BlockSpec((tm, QL), lambda i: (i, 0)),
                   pl.BlockSpec((tm, KVL), lambda i: (i, 0)),
                   pl.BlockSpec((tm, RP), lambda i: (i, 0)),
                   pl.BlockSpec((tm, RP), lambda i: (i, 0))],
        out_shape=[jax.ShapeDtypeStruct((N, QL), bf16),
                   jax.ShapeDtypeStruct((N, KVL), f32),
                   jax.ShapeDtypeStruct((N, RP), f32),
                   jax.ShapeDtypeStruct((N, RP), f32)],
        compiler_params=_cparams(("arbitrary",)),
        name="mla_down",
    )(y, modr, modr, w_a2, g_q.reshape(1, QL), g_kv.reshape(1, KVL), cos_t, sin_t)


def _q_call(cq, w_q, cos_t, sin_t, cfg):
    N, tm, H = cfg.N, cfg.TM, cfg.H
    QL, NP, RP = cfg.QL, cfg.NOPE, cfg.ROPE
    dk = NP + RP
    wq = NP + 2 * RP
    qscale = LOG2E / math.sqrt(dk)

    def kern(cq_ref, w_ref, cos_ref, sin_ref, q_ref):
        cq_t = cq_ref[...]
        cos, sin = cos_ref[...], sin_ref[...]
        for h in range(H):
            t = _dot(cq_t, w_ref[h])
            rope = t[:, NP:NP + RP] * cos + t[:, NP + RP:] * sin
            q_ref[h] = (jnp.concatenate([t[:, :NP], rope], axis=-1) * qscale).astype(bf16)

    return pl.pallas_call(
        kern,
        grid=(N // tm,),
        in_specs=[pl.BlockSpec((tm, QL), lambda i: (i, 0)),
                  pl.BlockSpec((H, QL, wq), lambda i: (0, 0, 0)),
                  pl.BlockSpec((tm, RP), lambda i: (i, 0)),
                  pl.BlockSpec((tm, RP), lambda i: (i, 0))],
        out_specs=pl.BlockSpec((H, tm, dk), lambda i: (0, i, 0)),
        out_shape=jax.ShapeDtypeStruct((H, N, dk), bf16),
        compiler_params=_cparams(("arbitrary",)),
        name="mla_q",
    )(cq, w_q, cos_t, sin_t)


def _kv_call(ckv_all, kr_all, w_kv, cfg):
    H, KVL, NP, RP, VD = cfg.H, cfg.KVL, cfg.NOPE, cfg.ROPE, cfg.VD
    LK, tkv, ns = cfg.LK, cfg.TKV, cfg.n_seg
    dk = NP + RP

    def kern(c_ref, kr_ref, w_ref, k_ref, v_ref):
        c = c_ref[0].astype(bf16)
        kr = kr_ref[0]
        for h in range(H):
            t = _dot(c, w_ref[h])
            k_ref[h, 0] = jnp.concatenate([t[:, :NP], kr], axis=-1).astype(bf16)
            v_ref[h, 0] = jnp.concatenate([t[:, NP:], jnp.ones((tkv, 128), f32)], axis=-1).astype(bf16)

    return pl.pallas_call(
        kern,
        grid=(ns, LK // tkv),
        in_specs=[pl.BlockSpec((1, tkv, KVL), lambda s, i: (s, i, 0)),
                  pl.BlockSpec((1, tkv, RP), lambda s, i: (s, i, 0)),
                  pl.BlockSpec((H, KVL, NP + VD), lambda s, i: (0, 0, 0))],
        out_specs=[pl.BlockSpec((H, 1, tkv, dk), lambda s, i: (0, s, i, 0)),
                   pl.BlockSpec((H, 1, tkv, VD + 128), lambda s, i: (0, s, i, 0))],
        out_shape=[jax.ShapeDtypeStruct((H, ns, LK, dk), bf16),
                   jax.ShapeDtypeStruct((H, ns, LK, VD + 128), bf16)],
        compiler_params=_cparams(("arbitrary", "arbitrary")),
        name="mla_kv",
    )(ckv_all, kr_all, w_kv)


def _attention_call(q, k, v, cfg):
    H, NP, RP, VD = cfg.H, cfg.NOPE, cfg.ROPE, cfg.VD
    assert VD == 128
    dk = NP + RP
    LK, TQ, SEQ, HB = cfg.LK, cfg.TQ, cfg.SEQ, cfg.HB
    nq = cfg.seg_len // TQ
    sub = TQ // SEQ

    def scores(qb, kb):
        return lax.dot_general(qb, kb, (((1,), (1,)), ((), ())), preferred_element_type=f32)

    def finish(s, vb):
        p = jnp.exp2(s - jnp.max(s, axis=-1, keepdims=True)).astype(bf16)
        acc = _dot(p, vb)
        return (acc[:, :VD] / acc[:, VD:]).astype(bf16)

    def kern(q_ref, k_ref, v_ref, o_ref):
        seg = pl.program_id(0)
        qi = pl.program_id(2)

        @pl.when(seg == 0)
        def _():
            for hh in range(HB):
                for j in range(sub):
                    off = pl.multiple_of((qi * sub + j) * SEQ, SEQ)
                    s = scores(q_ref[hh, j * SEQ:(j + 1) * SEQ, :], k_ref[hh, 0, pl.ds(off, SEQ), :])
                    o_ref[j * SEQ:(j + 1) * SEQ, hh * VD:(hh + 1) * VD] = finish(
                        s, v_ref[hh, 0, pl.ds(off, SEQ), :])

        @pl.when(seg != 0)
        def _():
            ss = [scores(q_ref[hh], k_ref[hh, 0]) for hh in range(HB)]
            for hh in range(HB):
                o_ref[:, hh * VD:(hh + 1) * VD] = finish(ss[hh], v_ref[hh, 0])

    return pl.pallas_call(
        kern,
        grid=(cfg.n_seg, H // HB, nq),
        in_specs=[pl.BlockSpec((HB, TQ, dk), lambda s, h, i: (h, s * nq + i, 0)),
                  pl.BlockSpec((HB, 1, LK, dk), lambda s, h, i: (h, s, 0, 0)),
                  pl.BlockSpec((HB, 1, LK, VD + 128), lambda s, h, i: (h, s, 0, 0))],
        out_specs=pl.BlockSpec((TQ, HB * VD), lambda s, h, i: (s * nq + i, h)),
        out_shape=jax.ShapeDtypeStruct((cfg.N, H * VD), bf16),
        compiler_params=_cparams(("arbitrary", "arbitrary", "arbitrary")),
        name="mla_attention",
    )(q, k, v)


def _router_call(y, modr, layer, wr2, b_router, cfg):
    N, D, tm, E, NG = cfg.N, cfg.D, cfg.TM, cfg.E, cfg.NG
    epg = E // NG

    def top2(x, jio):
        m1 = jnp.max(x, axis=0, keepdims=True)
        i1 = jnp.min(jnp.where(x == m1, jio, epg), axis=0, keepdims=True)
        x2 = jnp.where(jio == i1, -jnp.inf, x)
        m2 = jnp.max(x2, axis=0, keepdims=True)
        i2 = jnp.min(jnp.where(x2 == m2, jio, epg), axis=0, keepdims=True)
        return m1, i1, m2, i2

    def kern(y_ref, sh_ref, sc_ref, w_ref, b_ref, u_ref, e_ref, g_ref, cnt_ref):
        i = pl.program_id(0)
        u = _ln(y_ref[...]) * (1.0 + sc_ref[0]) + sh_ref[0]
        u_hi = u.astype(bf16)
        u_ref[...] = _pack2(u[:, :D // 2], u[:, D // 2:])
        u_lo = (u - u_hi.astype(f32)).astype(bf16)
        out = _dot(jnp.concatenate([u_hi, u_lo], axis=1), w_ref[...])
        lt = out.T
        logits = lt[:E] + lt[E:]
        scores = _sigmoid(logits)
        biased = scores + b_ref[...]
        jio = lax.broadcasted_iota(i32, (epg, tm), 0)
        best = gi = None
        for g in range(NG):
            m1, _, m2, _ = top2(biased[g * epg:(g + 1) * epg], jio)
            gs = m1 + m2
            if g == 0:
                best, gi = gs, jnp.zeros((1, tm), i32)
            else:
                better = gs > best
                best = jnp.where(better, gs, best)
                gi = jnp.where(better, g, gi)
        sel_b = biased[:epg]
        sel_s = scores[:epg]
        for g in range(1, NG):
            pick = gi == g
            sel_b = jnp.where(pick, biased[g * epg:(g + 1) * epg], sel_b)
            sel_s = jnp.where(pick, scores[g * epg:(g + 1) * epg], sel_s)
        _, i1, _, i2 = top2(sel_b, jio)
        s1 = jnp.sum(jnp.where(jio == i1, sel_s, 0.0), axis=0, keepdims=True)
        s2 = jnp.sum(jnp.where(jio == i2, sel_s, 0.0), axis=0, keepdims=True)
        den = s1 + s2
        e1 = gi * epg + i1
        e2 = gi * epg + i2
        e_ref[0:1, :] = e1
        e_ref[1:2, :] = e2
        rio = lax.broadcasted_iota(i32, (128, tm), 0)
        gates = jnp.where(rio == 0, ROUTED_SCALE * s1 / den,
                          jnp.where(rio == 1, ROUTED_SCALE * s2 / den, 0.0))
        g_ref[...] = gates.T
        eio = lax.broadcasted_iota(i32, (E, tm), 0)
        hits = jnp.where(eio == e1, 1.0, 0.0) + jnp.where(eio == e2, 1.0, 0.0)
        part = jnp.sum(hits, axis=1, keepdims=True)

        @pl.when(i == 0)
        def _():
            cnt_ref[...] = jnp.zeros_like(cnt_ref)

        cnt_ref[...] += part

    return pl.pallas_call(
        kern,
        grid=(N // tm,),
        in_specs=[pl.BlockSpec((tm, D), lambda i: (i, 0)),
                  _mod_spec(cfg, layer, 3, tm), _mod_spec(cfg, layer, 4, tm),
                  pl.BlockSpec((2 * D, 2 * E), lambda i: (0, 0)),
                  pl.BlockSpec((E, 1), lambda i: (0, 0))],
        out_specs=[pl.BlockSpec((tm, D // 2), lambda i: (i, 0)),
                   pl.BlockSpec((2, tm), lambda i: (0, i)),
                   pl.BlockSpec((tm, 128), lambda i: (i, 0)),
                   pl.BlockSpec((E, 128), lambda i: (0, 0))],
        out_shape=[jax.ShapeDtypeStruct((N, D // 2), jnp.uint32),
                   jax.ShapeDtypeStruct((2, N), i32),
                   jax.ShapeDtypeStruct((N, 128), f32),
                   jax.ShapeDtypeStruct((E, 128), f32)],
        compiler_params=_cparams(("arbitrary",)),
        name="moe_router",
    )(y, modr, modr, wr2, b_router.reshape(E, 1))


def _rank_call(e_flat, pstart, tri, cfg):
    E, C = cfg.E, cfg.RC
    n_asg = e_flat.shape[1]

    def kern(e_ref, ps_ref, tri_ref, d_ref, base_ref):
        @pl.when(pl.program_id(0) == 0)
        def _():
            base_ref[...] = jnp.zeros_like(base_ref)

        eio = lax.broadcasted_iota(i32, (E, C), 0)
        oh = eio == e_ref[...]
        incl = _dot(jnp.where(oh, 1.0, 0.0).astype(bf16), tri_ref[...])
        base = base_ref[:, 0:1]
        pos = ps_ref[...] + base + incl - 1.0
        d_ref[...] = jnp.sum(jnp.where(oh, pos, 0.0), axis=0, keepdims=True).astype(i32)
        base_ref[...] += incl[:, C - 1:C]

    return pl.pallas_call(
        kern,
        grid=(n_asg // C,),
        in_specs=[pl.BlockSpec((1, C), lambda i: (0, i)),
                  pl.BlockSpec((E, 1), lambda i: (0, 0)),
                  pl.BlockSpec((C, C), lambda i: (0, 0))],
        out_specs=pl.BlockSpec((1, C), lambda i: (0, i)),
        out_shape=jax.ShapeDtypeStruct((1, n_asg), i32),
        scratch_shapes=[pltpu.VMEM((E, 128), f32)],
        compiler_params=_cparams(("arbitrary",)),
        name="moe_rank",
    )(e_flat, pstart, tri)


def _dispatch_call(dest3, u, xs0, cfg):
    N, D, tm = cfg.N, cfg.D, cfg.TM

    def kern(d_ref, u_ref, xs_in_ref, xs_ref, sem):
        del xs_in_ref

        def issue(g, c):
            for j in range(8):
                for k in range(2):
                    pltpu.make_async_copy(u_ref.at[g, pl.ds(j, 1), :],
                                          xs_ref.at[pl.ds(d_ref[0, k, g * 8 + j], 1), :], sem).start()
            return c

        lax.fori_loop(0, tm // 8, issue, 0)
        for k in range(2):
            pltpu.make_async_copy(xs_ref.at[pl.ds(0, tm), :], xs_ref.at[pl.ds(0, tm), :], sem).wait()

    return pl.pallas_call(
        kern,
        grid=(N // tm,),
        in_specs=[pl.BlockSpec((1, 2, tm), lambda i: (i, 0, 0), memory_space=pltpu.SMEM),
                  pl.BlockSpec((tm // 8, 8, D // 2), lambda i: (i, 0, 0)),
                  pl.BlockSpec(memory_space=pl.ANY)],
        out_specs=pl.BlockSpec(memory_space=pl.ANY),
        out_shape=jax.ShapeDtypeStruct(xs0.shape, xs0.dtype),
        scratch_shapes=[pltpu.SemaphoreType.DMA(())],
        input_output_aliases={2: 0},
        compiler_params=_cparams(("arbitrary",)),
        name="moe_dispatch",
    )(dest3, u.reshape(N // 8, 8, D // 2), xs0)


def _expert_call(tile_e, tile_src, tile_used, tile_next, tile_slot, xs, w_exp_in, w_exp_out, layer, cfg):
    D, DE, TE = cfg.D, cfg.DE, cfg.TE
    n_rows = xs.shape[0]
    n_tiles = n_rows // TE

    def kern(te_ref, ts_ref, tu_ref, tn_ref, tl_ref, x_ref, wi_hbm, wo_hbm, y_ref,
             wi_f, wo_f, wi_bf, wo_bf, sem):
        i = pl.program_id(0)
        e = te_ref[i]
        slot = tl_ref[i]
        first = (i == 0) | (e != te_ref[jnp.maximum(i - 1, 0)])

        def copies(expert, s):
            return (pltpu.make_async_copy(wi_hbm.at[layer, expert], wi_f.at[s], sem.at[0, s]),
                    pltpu.make_async_copy(wo_hbm.at[layer, expert], wo_f.at[s], sem.at[1, s]))

        @pl.when(i == 0)
        def _():
            for cp in copies(e, slot):
                cp.start()

        @pl.when(first)
        def _():
            for cp in copies(e, slot):
                cp.wait()
            nxt = tn_ref[i]

            @pl.when(nxt >= 0)
            def _():
                for cp in copies(nxt, 1 - slot):
                    cp.start()

            wi_bf[...] = wi_f[slot].astype(bf16)
            wo_bf[...] = wo_f[slot].astype(bf16)

        @pl.when(tu_ref[i] == 1)
        def _():
            h = _dot(jnp.concatenate(_unpack2(x_ref[...]), axis=1), wi_bf[...])
            act = h[:, :DE] * _sigmoid(h[:, :DE]) * h[:, DE:]
            y_ref[...] = _dot(act.astype(bf16), wo_bf[...])

        @pl.when(tu_ref[i] == 0)
        def _():
            y_ref[...] = jnp.zeros_like(y_ref)

    grid_spec = pltpu.PrefetchScalarGridSpec(
        num_scalar_prefetch=5,
        grid=(n_tiles,),
        in_specs=[pl.BlockSpec((TE, D // 2), lambda i, te, ts, tu, tn, tl: (ts[i], 0)),
                  pl.BlockSpec(memory_space=pl.ANY),
                  pl.BlockSpec(memory_space=pl.ANY)],
        out_specs=pl.BlockSpec((TE, D), lambda i, te, ts, tu, tn, tl: (i, 0)),
        scratch_shapes=[pltpu.VMEM((2, D, 2 * DE), f32), pltpu.VMEM((2, DE, D), f32),
                        pltpu.VMEM((D, 2 * DE), bf16), pltpu.VMEM((DE, D), bf16),
                        pltpu.SemaphoreType.DMA((2, 2))],
    )
    return pl.pallas_call(
        kern,
        grid_spec=grid_spec,
        out_shape=jax.ShapeDtypeStruct((n_rows, D), f32),
        compiler_params=_cparams(("arbitrary",)),
        name="moe_experts",
    )(tile_e, tile_src, tile_used, tile_next, tile_slot, xs, w_exp_in, w_exp_out)


def _combine_call(dest3, y, gt, ys, modr, layer, ln_g, ln_b, cfg, split_out=False):
    N, D, tm = cfg.N, cfg.D, cfg.TM
    alpha = cfg.alpha
    spt = cfg.seg_len // tm
    n_out = 2 if split_out else 1

    def kern(d_ref, y_ref, gt_ref, g2_ref, lg_ref, lb_ref, ys_ref, *rest):
        o_refs, (r0, r1, sem) = rest[:n_out], rest[n_out:]
        bufs = (r0, r1)

        def issue(g, c):
            for j in range(8):
                for k in range(2):
                    pltpu.make_async_copy(ys_ref.at[pl.ds(d_ref[0, k, g * 8 + j], 1), :],
                                          bufs[k].at[g, pl.ds(j, 1), :], sem).start()
            return c

        lax.fori_loop(0, tm // 8, issue, 0)
        for k in range(2):
            pltpu.make_async_copy(ys_ref.at[pl.ds(0, tm), :], ys_ref.at[pl.ds(0, tm), :], sem).wait()
        gt_t = gt_ref[...]
        f = gt_t[:, 0:1] * r0[...].reshape(tm, D) + gt_t[:, 1:2] * r1[...].reshape(tm, D)
        z = alpha * y_ref[...] + g2_ref[0] * f
        res = _ln(z) * lg_ref[...] + lb_ref[...]
        if split_out:
            @pl.when(pl.program_id(0) < spt)
            def _():
                o_refs[0][...] = res

            @pl.when(pl.program_id(0) >= spt)
            def _():
                o_refs[1][...] = res
        else:
            o_refs[0][...] = res

    if split_out:
        out_specs = _rows_specs(2, tm, cfg, D)
        out_shape = [jax.ShapeDtypeStruct((cfg.seg_len, D), f32),
                     jax.ShapeDtypeStruct((N - cfg.seg_len, D), f32)]
    else:
        out_specs = pl.BlockSpec((tm, D), lambda i: (i, 0))
        out_shape = jax.ShapeDtypeStruct((N, D), f32)
    return pl.pallas_call(
        kern,
        grid=(N // tm,),
        in_specs=[pl.BlockSpec((1, 2, tm), lambda i: (i, 0, 0), memory_space=pltpu.SMEM),
                  pl.BlockSpec((tm, D), lambda i: (i, 0)),
                  pl.BlockSpec((tm, 128), lambda i: (i, 0)),
                  _mod_spec(cfg, layer, 5, tm),
                  pl.BlockSpec((1, D), lambda i: (0, 0)),
                  pl.BlockSpec((1, D), lambda i: (0, 0)),
                  pl.BlockSpec(memory_space=pl.ANY)],
        out_specs=out_specs,
        out_shape=out_shape,
        scratch_shapes=[pltpu.VMEM((tm // 8, 8, D), f32), pltpu.VMEM((tm // 8, 8, D), f32),
                        pltpu.SemaphoreType.DMA(())],
        compiler_params=_cparams(("arbitrary",)),
        name="moe_combine",
    )(dest3, y, gt, modr, ln_g.reshape(1, D), ln_b.reshape(1, D), ys)


def _moe_layer(y, xs_buf, modr, layer, wr2, b_router, tri, w_exp_in, w_exp_out, ln_g, ln_b, cfg, split_out):
    N, D, E, TE, tm = cfg.N, cfg.D, cfg.E, cfg.TE, cfg.TM
    u, e2, gt, cnt = _router_call(y, modr, layer, wr2, b_router, cfg)
    counts = cnt[:, 0].astype(i32)
    padded = (counts + TE - 1) // TE * TE
    pend = jnp.cumsum(padded)
    pstart = pend - padded
    n_rows = 2 * N + E * TE
    n_tiles = n_rows // TE
    tile_start = jnp.arange(n_tiles, dtype=i32) * TE
    n_used = pend[-1] // TE
    tile_src = jnp.minimum(jnp.arange(n_tiles, dtype=i32), n_used - 1)
    tile_e = jnp.minimum(jnp.sum((pend[None, :] <= (tile_src * TE)[:, None]).astype(i32), axis=1), E - 1)
    tile_used = (tile_start < pend[-1]).astype(i32)
    eids = jnp.arange(E, dtype=i32)
    nonempty = counts > 0
    later = (eids[None, :] > eids[:, None]) & nonempty[None, :]
    next_e = jnp.min(jnp.where(later, eids[None, :], E), axis=1)
    next_e = jnp.where(next_e >= E, -1, next_e)
    run_idx = jnp.cumsum(nonempty.astype(i32)) - 1
    pick = tile_e[:, None] == eids[None, :]
    tile_next = jnp.sum(jnp.where(pick, next_e[None, :], 0), axis=1)
    tile_slot = jnp.sum(jnp.where(pick, run_idx[None, :], 0), axis=1) % 2

    dest = _rank_call(e2.reshape(1, 2 * N), pstart.astype(f32).reshape(E, 1), tri, cfg)
    dest3 = dest.reshape(2, N // tm, tm).transpose(1, 0, 2)
    xs = _dispatch_call(dest3, u, xs_buf, cfg)
    ys = _expert_call(tile_e, tile_src, tile_used, tile_next, tile_slot, xs, w_exp_in, w_exp_out, layer, cfg)
    return _combine_call(dest3, y, gt, ys, modr, layer, ln_g, ln_b, cfg, split_out), xs


def _dft_tables(cfg):
    D, R, L = cfg.D, cfg.R, cfg.SEQ
    gd = D // cfg.F_GROUPS

    def cs(n, scale):
        k = np.arange(n)
        ang = 2.0 * np.pi * ((k[:, None] * k[None, :]) % n) / n
        return np.cos(ang) * scale, np.sin(ang) * scale

    cc, sc = cs(gd, 1.0 / math.sqrt(gd))
    chan = jnp.asarray(np.concatenate([cc, -sc], axis=1), bf16)
    cl, sl = cs(L, 1.0 / math.sqrt(L))
    cr, sr = cs(R, 1.0 / math.sqrt(R))
    J, KB = cfg.FFT_J, cfg.FFT_KB
    f2 = np.concatenate([cr, -sr], axis=0)
    k1m = jnp.asarray(np.kron(f2, np.eye(J)), bf16)
    eye = np.eye(KB)
    k2c = jnp.asarray(np.einsum("kn,ab->kabn", cr, eye).reshape(R * KB, KB * R), bf16)
    k2s = jnp.asarray(np.einsum("kn,ab->kabn", sr, eye).reshape(R * KB, KB * R), bf16)
    k1 = np.arange(R)
    ang = 2.0 * np.pi * (k1[:, None] * k1[None, :]) / (R * R)
    ang = ang.reshape(R, R // J, J).transpose(1, 0, 2).reshape(R // J, R * J)
    tw_re = jnp.asarray(np.repeat(np.cos(ang)[:, :, None], 128, axis=2), f32)
    tw_im = jnp.asarray(np.repeat(-np.sin(ang)[:, :, None], 128, axis=2), f32)
    return dict(chan=chan, c_seq=jnp.asarray(cl, bf16), s_seq=jnp.asarray(sl, bf16), k1m=k1m,
                k2c=k2c, k2s=k2s, tw_re=tw_re, tw_im=tw_im)


def _rot_cols(w):
    shp = w.shape
    w2 = w.reshape(shp[:-1] + (shp[-1] // 2, 2))
    return jnp.stack([-w2[..., 1], w2[..., 0]], axis=-1).reshape(shp)


def _rope_tables(cfg):
    L, RP = cfg.seg_len, cfg.ROPE
    pairs = RP // 4
    rows = L // cfg.GRID_W
    t_row = jnp.repeat(jnp.arange(rows, dtype=f32), cfg.GRID_W)
    t_col = jnp.tile(jnp.arange(cfg.GRID_W, dtype=f32), rows)
    inv = ROPE_BASE ** (-jnp.arange(pairs, dtype=f32) / pairs)
    ang = jnp.concatenate([t_row[:, None] * inv, t_col[:, None] * inv], -1)
    cos = jnp.repeat(jnp.cos(ang), 2, axis=-1)
    sin = jnp.repeat(jnp.sin(ang), 2, axis=-1)
    cos_t = jnp.concatenate([jnp.ones((L, RP), f32), jnp.tile(cos, (cfg.DEC_BATCH, 1))], 0)
    sin_t = jnp.concatenate([jnp.zeros((L, RP), f32), jnp.tile(sin, (cfg.DEC_BATCH, 1))], 0)
    return cos_t, sin_t


def _mla_weights(w_a, w_uq, w_ukv, cfg):
    D, H, QL, KVL, NP, RP, VD = cfg.D, cfg.H, cfg.QL, cfg.KVL, cfg.NOPE, cfg.ROPE, cfg.VD
    kr_w = w_a[:, QL + KVL:]
    z = jnp.zeros((D, 128 - RP), f32)
    w_a2 = jnp.concatenate([w_a[:, :QL + KVL], kr_w, z, _rot_cols(kr_w), z], axis=1).astype(bf16)
    wq = w_uq.reshape(QL, H, NP + RP)
    w_q = jnp.concatenate([wq, _rot_cols(wq[..., NP:])], axis=-1).transpose(1, 0, 2).astype(bf16)
    w_kv = w_ukv.reshape(KVL, H, NP + VD).transpose(1, 0, 2).astype(bf16)
    return w_a2, w_q, w_kv


def _forward(cfg, x_prompt, x_sample, c, cache_ckv, cache_krope, c_ctx, w_ada, b_ada, w_f_in, w_f_out,
             w_mla_a, g_mla_q, g_mla_kv, w_mla_uq, w_mla_ukv, w_mla_o, ln_mix_g, ln_mix_b,
             ln_ffn_g, ln_ffn_b, w_router, b_router, w_exp_in, w_exp_out):
    assert cfg.BATCH * cfg.SEQ == cfg.seg_len and cfg.R * cfg.R == cfg.seg_len
    assert cfg.TQ % cfg.SEQ == 0 and cfg.LK % cfg.TKV == 0
    assert cfg.R % cfg.FFT_J == 0 and cfg.R % cfg.FFT_KB == 0
    N, D, L, E = cfg.N, cfg.D, cfg.seg_len, cfg.E
    assert cfg.DEPTH >= 2
    y = (x_prompt.reshape(L, D), x_sample.reshape(cfg.DEC_BATCH * L, D))

    cvec = jnp.concatenate([c_ctx[None], c, jnp.zeros((MOD_ROWS - cfg.n_seg, D), f32)], axis=0)
    mod = _ada_call(cvec, w_ada, b_ada, cfg)
    modr = mod.reshape(cfg.DEPTH * MOD_ROWS * 6, 1, D)

    tabs = _dft_tables(cfg)
    cos_t, sin_t = _rope_tables(cfg)
    w_hi = w_router.astype(bf16)
    w_lo = (w_router - w_hi.astype(f32)).astype(bf16)
    wr2 = jnp.concatenate([jnp.concatenate([w_hi, w_lo], 1),
                           jnp.concatenate([w_hi, jnp.zeros_like(w_lo)], 1)], 0)
    rc = cfg.RC
    tri = jnp.asarray(np.triu(np.ones((rc, rc), np.float32)), bf16)

    ckv_out, kr_out = [], []
    xs_buf = jnp.zeros((2 * N + E * cfg.TE, D // 2), jnp.uint32)
    for l in range(cfg.DEPTH):
        j = l // 2
        if l % 2 == 0:
            yt = y if isinstance(y, tuple) else (y,)
            a = _fourier_in_call(yt, modr, l, w_f_in[j].astype(bf16), tabs["chan"], cfg)
            f_p = _prompt_dft_call(a, tabs["c_seq"], tabs["s_seq"], cfg)
            t = _fft_stage1_call(a, tabs["k1m"], tabs["tw_re"], tabs["tw_im"], cfg)
            f_s = _fft_stage2_call(t, tabs["k2c"], tabs["k2s"], cfg)
            y = _post_call((f_p, f_s), w_f_out[j].astype(bf16), yt, modr, l, 2, ln_mix_g[l], ln_mix_b[l], cfg)
        else:
            w_a2, w_q, w_kv = _mla_weights(w_mla_a[j], w_mla_uq[j], w_mla_ukv[j], cfg)
            cq, ckv, kr, krr = _mla_a_call(y, modr, l, w_a2, g_mla_q[j], g_mla_kv[j], cos_t, sin_t, cfg)
            ckv_out.append(ckv[:L].reshape(cfg.BATCH, cfg.SEQ, cfg.KVL))
            kr_out.append(kr[:L].reshape(cfg.BATCH, cfg.SEQ, cfg.ROPE))
            q = _q_call(cq, w_q, cos_t, sin_t, cfg)
            pad = cfg.PAST
            ckv_all = jnp.concatenate([
                jnp.concatenate([ckv[:L], jnp.zeros((pad, cfg.KVL), f32)], 0)[None],
                jnp.concatenate([cache_ckv[:, j], ckv[L:].reshape(cfg.DEC_BATCH, L, cfg.KVL)], 1)], 0)
            kr_all = jnp.concatenate([
                jnp.concatenate([krr[:L], jnp.zeros((pad, cfg.ROPE), f32)], 0)[None],
                jnp.concatenate([cache_krope[:, j], krr[L:].reshape(cfg.DEC_BATCH, L, cfg.ROPE)], 1)], 0)
            k, v = _kv_call(ckv_all, kr_all, w_kv, cfg)
            o = _attention_call(q, k, v, cfg)
            y = _post_call((o,), w_mla_o[j].astype(bf16), (y,), modr, l, 2, ln_mix_g[l], ln_mix_b[l], cfg)
        y, xs_buf = _moe_layer(y, xs_buf, modr, l, wr2, b_router, tri, w_exp_in, w_exp_out, ln_ffn_g[l],
                               ln_ffn_b[l], cfg, split_out=(l == cfg.DEPTH - 1))

    y_p = y[0].reshape(cfg.BATCH, cfg.SEQ, D)
    y_s = y[1].reshape(cfg.DEC_BATCH, L, D)
    return y_p, y_s, jnp.stack(ckv_out, axis=1), jnp.stack(kr_out, axis=1)


def kernel(x_prompt, x_sample, c, cache_ckv, cache_krope, c_ctx, w_ada, b_ada, w_f_in, w_f_out, w_mla_a, g_mla_q, g_mla_kv, w_mla_uq, w_mla_ukv, w_mla_o, ln_mix_g, ln_mix_b, ln_ffn_g, ln_ffn_b, w_router, b_router, w_exp_in, w_exp_out):
    return _forward(Cfg(), x_prompt, x_sample, c, cache_ckv, cache_krope, c_ctx, w_ada, b_ada, w_f_in,
                    w_f_out, w_mla_a, g_mla_q, g_mla_kv, w_mla_uq, w_mla_ukv, w_mla_o, ln_mix_g,
                    ln_mix_b, ln_ffn_g, ln_ffn_b, w_router, b_router, w_exp_in, w_exp_out)
```

```python
import functools
import math
from typing import NamedTuple

import numpy as np
import jax
import jax.numpy as jnp
from jax import lax
from jax.experimental import pallas as pl
from jax.experimental.pallas import tpu as pltpu

f32 = jnp.float32
bf16 = jnp.bfloat16
i32 = jnp.int32

LN_EPS = 1e-5
RMS_EPS = 1e-6
ROPE_BASE = 10000.0
ROUTED_SCALE = 2.5
LOG2E = 1.4426950408889634
NEG_BIG = -1e30
MOD_ROWS = 16
VMEM_LIMIT_V7X = 56 * 1024 * 1024


class Cfg(NamedTuple):
    D: int = 2048
    DEPTH: int = 4
    BATCH: int = 16
    SEQ: int = 256
    DEC_BATCH: int = 8
    DEC_SEQ: int = 4096
    PAST: int = 256
    GRID_W: int = 64
    F_GROUPS: int = 4
    H: int = 16
    QL: int = 512
    KVL: int = 512
    NOPE: int = 128
    ROPE: int = 64
    VD: int = 128
    E: int = 64
    NG: int = 8
    DE: int = 512
    TM: int = 512
    TN_ADA: int = 1024
    FFT_J: int = 8
    FFT_KB: int = 8
    TKV: int = 256
    TQ: int = 512
    HB: int = 2
    RC: int = 1024
    TE: int = 256

    @property
    def seg_len(self):
        return self.DEC_SEQ

    @property
    def n_seg(self):
        return 1 + self.DEC_BATCH

    @property
    def N(self):
        return self.n_seg * self.seg_len

    @property
    def R(self):
        return int(round(math.sqrt(self.seg_len)))

    @property
    def LK(self):
        return self.PAST + self.seg_len

    @property
    def alpha(self):
        return (2 * self.DEPTH) ** 0.25


def _cparams(sem):
    return pltpu.CompilerParams(dimension_semantics=sem, vmem_limit_bytes=VMEM_LIMIT_V7X)


def _ln(x):
    mu = jnp.mean(x, axis=-1, keepdims=True)
    xc = x - mu
    var = jnp.mean(xc * xc, axis=-1, keepdims=True)
    return xc * lax.rsqrt(var + LN_EPS)


def _sigmoid(x):
    return 1.0 / (1.0 + jnp.exp(-x))


def _dot(a, b):
    return jnp.dot(a, b, preferred_element_type=f32)


def _pack2(hi, lo):
    h = lax.bitcast_convert_type(hi.astype(bf16).astype(f32), jnp.uint32)
    l = lax.bitcast_convert_type(lo.astype(bf16).astype(f32), jnp.uint32)
    return h | (l >> 16)


def _unpack2(w):
    hi = lax.bitcast_convert_type(w & jnp.uint32(0xFFFF0000), f32).astype(bf16)
    lo = lax.bitcast_convert_type(w << 16, f32).astype(bf16)
    return hi, lo


def _mod_spec(cfg, layer, which, tm):
    spt = cfg.seg_len // tm
    base = layer * MOD_ROWS * 6
    return pl.BlockSpec((1, 1, cfg.D), lambda i: (base + (i // spt) * 6 + which, 0, 0))


def _rows_specs(n_arrs, tm, cfg, width):
    spt = cfg.seg_len // tm
    if n_arrs == 2:
        return [pl.BlockSpec((tm, width), lambda i: (jnp.minimum(i, spt - 1), 0)),
                pl.BlockSpec((tm, width), lambda i: (jnp.maximum(i - spt, 0), 0))]
    return [pl.BlockSpec((tm, width), lambda i: (i, 0))]


def _rows_read(refs, spt):
    if len(refs) == 2:
        return jnp.where(pl.program_id(0) < spt, refs[0][...], refs[1][...])
    return refs[0][...]


def _ada_call(cvec, w_ada, b_ada, cfg):
    D, n6, tn = cfg.D, 6 * cfg.D, cfg.TN_ADA

    def kern(c_ref, w_ref, b_ref, o_ref):
        c = c_ref[...]
        s = (c * _sigmoid(c)).astype(bf16)
        o_ref[0] = _dot(s, w_ref[0].astype(bf16)) + b_ref[0]

    return pl.pallas_call(
        kern,
        grid=(cfg.DEPTH, n6 // tn),
        in_specs=[pl.BlockSpec((MOD_ROWS, D), lambda l, j: (0, 0)),
                  pl.BlockSpec((1, D, tn), lambda l, j: (l, 0, j)),
                  pl.BlockSpec((1, 1, tn), lambda l, j: (l, 0, j))],
        out_specs=pl.BlockSpec((1, MOD_ROWS, tn), lambda l, j: (l, 0, j)),
        out_shape=jax.ShapeDtypeStruct((cfg.DEPTH, MOD_ROWS, n6), f32),
        compiler_params=_cparams(("arbitrary", "arbitrary")),
        name="ada",
    )(cvec, w_ada, b_ada.reshape(cfg.DEPTH, 1, n6))


def _fourier_in_call(y, modr, layer, w_in, cs, cfg):
    N, D, tm = cfg.N, cfg.D, cfg.TM
    gd = D // cfg.F_GROUPS
    ny = len(y)
    spt = cfg.seg_len // tm

    def kern(*refs):
        y_refs, (sh_ref, sc_ref, w_ref, cs_ref, a_ref) = refs[:ny], refs[ny:]
        u = _ln(_rows_read(y_refs, spt)) * (1.0 + sc_ref[0]) + sh_ref[0]
        h = _dot(u.astype(bf16), w_ref[...]).astype(bf16)
        for g in range(cfg.F_GROUPS):
            a = _dot(h[:, g * gd:(g + 1) * gd], cs_ref[...])
            a_ref[:, g * gd:(g + 1) * gd] = _pack2(a[:, :gd], a[:, gd:])

    return pl.pallas_call(
        kern,
        grid=(N // tm,),
        in_specs=_rows_specs(ny, tm, cfg, D) + [
            _mod_spec(cfg, layer, 0, tm), _mod_spec(cfg, layer, 1, tm),
            pl.BlockSpec((D, D), lambda i: (0, 0)),
            pl.BlockSpec((gd, 2 * gd), lambda i: (0, 0))],
        out_specs=pl.BlockSpec((tm, D), lambda i: (i, 0)),
        out_shape=jax.ShapeDtypeStruct((N, D), jnp.uint32),
        compiler_params=_cparams(("arbitrary",)),
        name="fourier_in",
    )(*y, modr, modr, w_in, cs)


def _prompt_dft_call(a, c_seq, s_seq, cfg):
    D, L = cfg.D, cfg.SEQ
    gd = D // cfg.F_GROUPS
    a3 = a.reshape(cfg.N // L, L, D)

    def kern(c_ref, s_ref, x_ref, o_ref):
        re, im = _unpack2(x_ref[0])
        o_ref[0] = _dot(c_ref[...], re) + _dot(s_ref[...], im)

    out = pl.pallas_call(
        kern,
        grid=(cfg.BATCH, cfg.F_GROUPS),
        in_specs=[pl.BlockSpec((L, L), lambda s, g: (0, 0)),
                  pl.BlockSpec((L, L), lambda s, g: (0, 0)),
                  pl.BlockSpec((1, L, gd), lambda s, g: (s, 0, g))],
        out_specs=pl.BlockSpec((1, L, gd), lambda s, g: (s, 0, g)),
        out_shape=jax.ShapeDtypeStruct((cfg.BATCH, L, D), f32),
        compiler_params=_cparams(("arbitrary", "arbitrary")),
        name="prompt_dft",
    )(c_seq, s_seq, a3)
    return out.reshape(cfg.seg_len, D)


def _fft_stage1_call(a, k1m, tw_re, tw_im, cfg):
    D, R, J = cfg.D, cfg.R, cfg.FFT_J
    gd = D // cfg.F_GROUPS
    rj = R * J
    a4 = a.reshape(cfg.n_seg, R, R, D)

    def kern(km_ref, twr_ref, twi_ref, x_ref, o_ref):
        x = jnp.concatenate(_unpack2(x_ref[0].reshape(rj, gd)), axis=1)
        pq = _dot(km_ref[...], x)
        p, q = pq[:rj], pq[rj:]
        t_re = p[:, :gd] - q[:, gd:]
        t_im = q[:, :gd] + p[:, gd:]
        twr = jnp.tile(twr_ref[0], (1, gd // 128))
        twi = jnp.tile(twi_ref[0], (1, gd // 128))
        o_ref[0] = _pack2(t_re * twr - t_im * twi, t_re * twi + t_im * twr).reshape(R, J, gd)

    return pl.pallas_call(
        kern,
        grid=(cfg.DEC_BATCH, R // J, cfg.F_GROUPS),
        in_specs=[pl.BlockSpec((2 * rj, rj), lambda b, j, g: (0, 0)),
                  pl.BlockSpec((1, rj, 128), lambda b, j, g: (j, 0, 0)),
                  pl.BlockSpec((1, rj, 128), lambda b, j, g: (j, 0, 0)),
                  pl.BlockSpec((1, R, J, gd), lambda b, j, g: (b + 1, 0, j, g))],
        out_specs=pl.BlockSpec((1, R, J, gd), lambda b, j, g: (b, 0, j, g)),
        out_shape=jax.ShapeDtypeStruct((cfg.DEC_BATCH, R, R, D), jnp.uint32),
        compiler_params=_cparams(("arbitrary", "arbitrary", "arbitrary")),
        name="fft_stage1",
    )(k1m, tw_re, tw_im, a4)


def _fft_stage2_call(t4, k2c, k2s, cfg):
    D, R, KB = cfg.D, cfg.R, cfg.FFT_KB
    gd = D // cfg.F_GROUPS
    t3 = t4.reshape(cfg.DEC_BATCH, R * R, D)

    def kern(kc_ref, ks_ref, x_ref, o_ref):
        re, im = _unpack2(x_ref[0])
        yv = _dot(kc_ref[...], re) + _dot(ks_ref[...], im)
        o_ref[0] = yv.reshape(R, KB, gd)

    out = pl.pallas_call(
        kern,
        grid=(cfg.DEC_BATCH, R // KB, cfg.F_GROUPS),
        in_specs=[pl.BlockSpec((R * KB, KB * R), lambda b, k, g: (0, 0)),
                  pl.BlockSpec((R * KB, KB * R), lambda b, k, g: (0, 0)),
                  pl.BlockSpec((1, KB * R, gd), lambda b, k, g: (b, k, g))],
        out_specs=pl.BlockSpec((1, R, KB, gd), lambda b, k, g: (b, 0, k, g)),
        out_shape=jax.ShapeDtypeStruct((cfg.DEC_BATCH, R, R, D), f32),
        compiler_params=_cparams(("arbitrary", "arbitrary", "arbitrary")),
        name="fft_stage2",
    )(k2c, k2s, t3)
    return out.reshape(cfg.DEC_BATCH * cfg.seg_len, D)


def _post_call(fs, w_out, y, modr, layer, which, ln_g, ln_b, cfg):
    N, D, tm = cfg.N, cfg.D, cfg.TM
    spt = cfg.seg_len // tm
    nf, ny = len(fs), len(y)
    alpha = cfg.alpha

    def kern(*refs):
        f_refs, w_ref, y_refs = refs[:nf], refs[nf], refs[nf + 1:nf + 1 + ny]
        g_ref, lg_ref, lb_ref, o_ref = refs[nf + 1 + ny:]
        m = _dot(_rows_read(f_refs, spt).astype(bf16), w_ref[...])
        z = alpha * _rows_read(y_refs, spt) + g_ref[0] * m
        o_ref[...] = _ln(z) * lg_ref[...] + lb_ref[...]

    return pl.pallas_call(
        kern,
        grid=(N // tm,),
        in_specs=_rows_specs(nf, tm, cfg, D) + [pl.BlockSpec((D, D), lambda i: (0, 0))]
        + _rows_specs(ny, tm, cfg, D) + [_mod_spec(cfg, layer, which, tm),
                                         pl.BlockSpec((1, D), lambda i: (0, 0)),
                                         pl.BlockSpec((1, D), lambda i: (0, 0))],
        out_specs=pl.BlockSpec((tm, D), lambda i: (i, 0)),
        out_shape=jax.ShapeDtypeStruct((N, D), f32),
        compiler_params=_cparams(("arbitrary",)),
        name="mixer_post",
    )(*fs, w_out, *y, modr, ln_g.reshape(1, D), ln_b.reshape(1, D))


def _mla_a_call(y, modr, layer, w_a2, g_q, g_kv, cos_t, sin_t, cfg):
    N, D, tm = cfg.N, cfg.D, cfg.TM
    QL, KVL, RP = cfg.QL, cfg.KVL, cfg.ROPE
    wa = w_a2.shape[1]
    o_kr = QL + KVL
    o_rot = o_kr + 128

    def rms(x, g):
        return x * lax.rsqrt(jnp.mean(x * x, axis=-1, keepdims=True) + RMS_EPS) * g

    def kern(y_ref, sh_ref, sc_ref, w_ref, gq_ref, gkv_ref, cos_ref, sin_ref,
             cq_ref, ckv_ref, kr_ref, krr_ref):
        u = _ln(y_ref[...]) * (1.0 + sc_ref[0]) + sh_ref[0]
        a = _dot(u.astype(bf16), w_ref[...])
        cq_ref[...] = rms(a[:, :QL], gq_ref[...]).astype(bf16)
        ckv_ref[...] = rms(a[:, QL:QL + KVL], gkv_ref[...])
        kr = a[:, o_kr:o_kr + RP]
        kr_ref[...] = kr
        krr_ref[...] = kr * cos_ref[...] + a[:, o_rot:o_rot + RP] * sin_ref[...]

    return pl.pallas_call(
        kern,
        grid=(N // tm,),
        in_specs=[pl.BlockSpec((tm, D), lambda i: (i, 0)),
                  _mod_spec(cfg, layer, 0, tm), _mod_spec(cfg, layer, 1, tm),
                  pl.BlockSpec((D, wa), lambda i: (0, 0)),
                  pl.BlockSpec((1, QL), lambda i: (0, 0)),
                  pl.BlockSpec((1, KVL), lambda i: (0, 0)),
                  pl.BlockSpec((tm, RP), lambda i: (i, 0)),
                  pl.BlockSpec((tm, RP), lambda i: (i, 0))],
        out_specs=[pl.BlockSpec((tm, QL), lambda i: (i, 0)),
                   pl.BlockSpec((tm, KVL), lambda i: (i, 0)),
                   pl.BlockSpec((tm, RP), lambda i: (i, 0)),
                   pl.BlockSpec((tm, RP), lambda i: (i, 0))],
        out_shape=[jax.ShapeDtypeStruct((N, QL), bf16),
                   jax.ShapeDtypeStruct((N, KVL), f32),
                   jax.ShapeDtypeStruct((N, RP), f32),
                   jax.ShapeDtypeStruct((N, RP), f32)],
        compiler_params=_cparams(("arbitrary",)),
        name="mla_down",
    )(y, modr, modr, w_a2, g_q.reshape(1, QL), g_kv.reshape(1, KVL), cos_t, sin_t)


def _q_call(cq, w_q, cos_t, sin_t, cfg):
    N, tm, H = cfg.N, cfg.TM, cfg.H
    QL, NP, RP = cfg.QL, cfg.NOPE, cfg.ROPE
    dk = NP + RP
    wq = NP + 2 * RP
    qscale = LOG2E / math.sqrt(dk)

    def kern(cq_ref, w_ref, cos_ref, sin_ref, q_ref):
        cq_t = cq_ref[...]
        cos, sin = cos_ref[...], sin_ref[...]
        for h in range(H):
            t = _dot(cq_t, w_ref[h])
            rope = t[:, NP:NP + RP] * cos + t[:, NP + RP:] * sin
            q_ref[h] = (jnp.concatenate([t[:, :NP], rope], axis=-1) * qscale).astype(bf16)

    return pl.pallas_call(
        kern,
        grid=(N // tm,),
        in_specs=[pl.BlockSpec((tm, QL), lambda i: (i, 0)),
                  pl.BlockSpec((H, QL, wq), lambda i: (0, 0, 0)),
                  pl.BlockSpec((tm, RP), lambda i: (i, 0)),
                  pl.BlockSpec((tm, RP), lambda i: (i, 0))],
        out_specs=pl.BlockSpec((H, tm, dk), lambda i: (0, i, 0)),
        out_shape=jax.ShapeDtypeStruct((H, N, dk), bf16),
        compiler_params=_cparams(("arbitrary",)),
        name="mla_q",
    )(cq, w_q, cos_t, sin_t)


def _kv_call(ckv_all, kr_all, w_kv, cfg):
    H, KVL, NP, RP, VD = cfg.H, cfg.KVL, cfg.NOPE, cfg.ROPE, cfg.VD
    LK, tkv, ns = cfg.LK, cfg.TKV, cfg.n_seg
    dk = NP + RP

    def kern(c_ref, kr_ref, w_ref, k_ref, v_ref):
        c = c_ref[0].astype(bf16)
        kr = kr_ref[0]
        for h in range(H):
            t = _dot(c, w_ref[h])
            k_ref[h, 0] = jnp.concatenate([t[:, :NP], kr], axis=-1).astype(bf16)
            v_ref[h, 0] = jnp.concatenate([t[:, NP:], jnp.ones((tkv, 128), f32)], axis=-1).astype(bf16)

    return pl.pallas_call(
        kern,
        grid=(ns, LK // tkv),
        in_specs=[pl.BlockSpec((1, tkv, KVL), lambda s, i: (s, i, 0)),
                  pl.BlockSpec((1, tkv, RP), lambda s, i: (s, i, 0)),
                  pl.BlockSpec((H, KVL, NP + VD), lambda s, i: (0, 0, 0))],
        out_specs=[pl.BlockSpec((H, 1, tkv, dk), lambda s, i: (0, s, i, 0)),
                   pl.BlockSpec((H, 1, tkv, VD + 128), lambda s, i: (0, s, i, 0))],
        out_shape=[jax.ShapeDtypeStruct((H, ns, LK, dk), bf16),
                   jax.ShapeDtypeStruct((H, ns, LK, VD + 128), bf16)],
        compiler_params=_cparams(("arbitrary", "arbitrary")),
        name="mla_kv",
    )(ckv_all, kr_all, w_kv)


def _attention_call(q, k, v, cfg):
    H, NP, RP, VD = cfg.H, cfg.NOPE, cfg.ROPE, cfg.VD
    assert VD == 128
    dk = NP + RP
    LK, TQ, SEQ, HB = cfg.LK, cfg.TQ, cfg.SEQ, cfg.HB
    nq = cfg.seg_len // TQ
    sub = TQ // SEQ

    def scores(qb, kb):
        return lax.dot_general(qb, kb, (((1,), (1,)), ((), ())), preferred_element_type=f32)

    def finish(s, vb):
        p = jnp.exp2(s - jnp.max(s, axis=-1, keepdims=True)).astype(bf16)
        acc = _dot(p, vb)
        return (acc[:, :VD] / acc[:, VD:]).astype(bf16)

    def kern(q_ref, k_ref, v_ref, o_ref):
        seg = pl.program_id(0)
        qi = pl.program_id(2)

        @pl.when(seg == 0)
        def _():
            for hh in range(HB):
                for j in range(sub):
                    off = pl.multiple_of((qi * sub + j) * SEQ, SEQ)
                    s = scores(q_ref[hh, j * SEQ:(j + 1) * SEQ, :], k_ref[hh, 0, pl.ds(off, SEQ), :])
                    o_ref[j * SEQ:(j + 1) * SEQ, hh * VD:(hh + 1) * VD] = finish(
                        s, v_ref[hh, 0, pl.ds(off, SEQ), :])

        @pl.when(seg != 0)
        def _():
            ss = [scores(q_ref[hh], k_ref[hh, 0]) for hh in range(HB)]
            for hh in range(HB):
                o_ref[:, hh * VD:(hh + 1) * VD] = finish(ss[hh], v_ref[hh, 0])

    return pl.pallas_call(
        kern,
        grid=(cfg.n_seg, H // HB, nq),
        in_specs=[pl.BlockSpec((HB, TQ, dk), lambda s, h, i: (h, s * nq + i, 0)),
                  pl.BlockSpec((HB, 1, LK, dk), lambda s, h, i: (h, s, 0, 0)),
                  pl.BlockSpec((HB, 1, LK, VD + 128), lambda s, h, i: (h, s, 0, 0))],
        out_specs=pl.BlockSpec((TQ, HB * VD), lambda s, h, i: (s * nq + i, h)),
        out_shape=jax.ShapeDtypeStruct((cfg.N, H * VD), bf16),
        compiler_params=_cparams(("arbitrary", "arbitrary", "arbitrary")),
        name="mla_attention",
    )(q, k, v)


def _router_call(y, modr, layer, wr2, b_router, cfg):
    N, D, tm, E, NG = cfg.N, cfg.D, cfg.TM, cfg.E, cfg.NG
    epg = E // NG

    def top2(x, jio):
        m1 = jnp.max(x, axis=0, keepdims=True)
        i1 = jnp.min(jnp.where(x == m1, jio, epg), axis=0, keepdims=True)
        x2 = jnp.where(jio == i1, -jnp.inf, x)
        m2 = jnp.max(x2, axis=0, keepdims=True)
        i2 = jnp.min(jnp.where(x2 == m2, jio, epg), axis=0, keepdims=True)
        return m1, i1, m2, i2

    def kern(y_ref, sh_ref, sc_ref, w_ref, b_ref, u_ref, e_ref, g_ref, cnt_ref):
        i = pl.program_id(0)
        u = _ln(y_ref[...]) * (1.0 + sc_ref[0]) + sh_ref[0]
        u_hi = u.astype(bf16)
        u_ref[...] = _pack2(u[:, :D // 2], u[:, D // 2:])
        u_lo = (u - u_hi.astype(f32)).astype(bf16)
        out = _dot(jnp.concatenate([u_hi, u_lo], axis=1), w_ref[...])
        lt = out.T
        logits = lt[:E] + lt[E:]
        scores = _sigmoid(logits)
        biased = scores + b_ref[...]
        jio = lax.broadcasted_iota(i32, (epg, tm), 0)
        best = gi = None
        for g in range(NG):
            m1, _, m2, _ = top2(biased[g * epg:(g + 1) * epg], jio)
            gs = m1 + m2
            if g == 0:
                best, gi = gs, jnp.zeros((1, tm), i32)
            else:
                better = gs > best
                best = jnp.where(better, gs, best)
                gi = jnp.where(better, g, gi)
        sel_b = biased[:epg]
        sel_s = scores[:epg]
        for g in range(1, NG):
            pick = gi == g
            sel_b = jnp.where(pick, biased[g * epg:(g + 1) * epg], sel_b)
            sel_s = jnp.where(pick, scores[g * epg:(g + 1) * epg], sel_s)
        _, i1, _, i2 = top2(sel_b, jio)
        s1 = jnp.sum(jnp.where(jio == i1, sel_s, 0.0), axis=0, keepdims=True)
        s2 = jnp.sum(jnp.where(jio == i2, sel_s, 0.0), axis=0, keepdims=True)
        den = s1 + s2
        e1 = gi * epg + i1
        e2 = gi * epg + i2
        e_ref[0:1, :] = e1
        e_ref[1:2, :] = e2
        rio = lax.broadcasted_iota(i32, (128, tm), 0)
        gates = jnp.where(rio == 0, ROUTED_SCALE * s1 / den,
                          jnp.where(rio == 1, ROUTED_SCALE * s2 / den, 0.0))
        g_ref[...] = gates.T
        eio = lax.broadcasted_iota(i32, (E, tm), 0)
        hits = jnp.where(eio == e1, 1.0, 0.0) + jnp.where(eio == e2, 1.0, 0.0)
        part = jnp.sum(hits, axis=1, keepdims=True)

        @pl.when(i == 0)
        def _():
            cnt_ref[...] = jnp.zeros_like(cnt_ref)

        cnt_ref[...] += part

    return pl.pallas_call(
        kern,
        grid=(N // tm,),
        in_specs=[pl.BlockSpec((tm, D), lambda i: (i, 0)),
                  _mod_spec(cfg, layer, 3, tm), _mod_spec(cfg, layer, 4, tm),
                  pl.BlockSpec((2 * D, 2 * E), lambda i: (0, 0)),
                  pl.BlockSpec((E, 1), lambda i: (0, 0))],
        out_specs=[pl.BlockSpec((tm, D // 2), lambda i: (i, 0)),
                   pl.BlockSpec((2, tm), lambda i: (0, i)),
                   pl.BlockSpec((tm, 128), lambda i: (i, 0)),
                   pl.BlockSpec((E, 128), lambda i: (0, 0))],
        out_shape=[jax.ShapeDtypeStruct((N, D // 2), jnp.uint32),
                   jax.ShapeDtypeStruct((2, N), i32),
                   jax.ShapeDtypeStruct((N, 128), f32),
                   jax.ShapeDtypeStruct((E, 128), f32)],
        compiler_params=_cparams(("arbitrary",)),
        name="moe_router",
    )(y, modr, modr, wr2, b_router.reshape(E, 1))


def _rank_call(e_flat, pstart, tri, cfg):
    E, C = cfg.E, cfg.RC
    n_asg = e_flat.shape[1]

    def kern(e_ref, ps_ref, tri_ref, d_ref, base_ref):
        @pl.when(pl.program_id(0) == 0)
        def _():
            base_ref[...] = jnp.zeros_like(base_ref)

        eio = lax.broadcasted_iota(i32, (E, C), 0)
        oh = eio == e_ref[...]
        incl = _dot(jnp.where(oh, 1.0, 0.0).astype(bf16), tri_ref[...])
        base = base_ref[:, 0:1]
        pos = ps_ref[...] + base + incl - 1.0
        d_ref[...] = jnp.sum(jnp.where(oh, pos, 0.0), axis=0, keepdims=True).astype(i32)
        base_ref[...] += incl[:, C - 1:C]

    return pl.pallas_call(
        kern,
        grid=(n_asg // C,),
        in_specs=[pl.BlockSpec((1, C), lambda i: (0, i)),
                  pl.BlockSpec((E, 1), lambda i: (0, 0)),
                  pl.BlockSpec((C, C), lambda i: (0, 0))],
        out_specs=pl.BlockSpec((1, C), lambda i: (0, i)),
        out_shape=jax.ShapeDtypeStruct((1, n_asg), i32),
        scratch_shapes=[pltpu.VMEM((E, 128), f32)],
        compiler_params=_cparams(("arbitrary",)),
        name="moe_rank",
    )(e_flat, pstart, tri)


def _dispatch_call(dest3, u, xs0, cfg):
    N, D, tm = cfg.N, cfg.D, cfg.TM

    def kern(d_ref, u_ref, xs_in_ref, xs_ref, sem):
        del xs_in_ref

        def issue(g, c):
            for j in range(8):
                for k in range(2):
                    pltpu.make_async_copy(u_ref.at[g, pl.ds(j, 1), :],
                                          xs_ref.at[pl.ds(d_ref[0, k, g * 8 + j], 1), :], sem).start()
            return c

        lax.fori_loop(0, tm // 8, issue, 0)
        for k in range(2):
            pltpu.make_async_copy(xs_ref.at[pl.ds(0, tm), :], xs_ref.at[pl.ds(0, tm), :], sem).wait()

    return pl.pallas_call(
        kern,
        grid=(N // tm,),
        in_specs=[pl.BlockSpec((1, 2, tm), lambda i: (i, 0, 0), memory_space=pltpu.SMEM),
                  pl.BlockSpec((tm // 8, 8, D // 2), lambda i: (i, 0, 0)),
                  pl.BlockSpec(memory_space=pl.ANY)],
        out_specs=pl.BlockSpec(memory_space=pl.ANY),
        out_shape=jax.ShapeDtypeStruct(xs0.shape, xs0.dtype),
        scratch_shapes=[pltpu.SemaphoreType.DMA(())],
        input_output_aliases={2: 0},
        compiler_params=_cparams(("arbitrary",)),
        name="moe_dispatch",
    )(dest3, u.reshape(N // 8, 8, D // 2), xs0)


def _expert_call(tile_e, tile_src, tile_used, tile_next, tile_slot, xs, w_exp_in, w_exp_out, layer, cfg):
    D, DE, TE = cfg.D, cfg.DE, cfg.TE
    n_rows = xs.shape[0]
    n_tiles = n_rows // TE

    def kern(te_ref, ts_ref, tu_ref, tn_ref, tl_ref, x_ref, wi_hbm, wo_hbm, y_ref,
             wi_f, wo_f, wi_bf, wo_bf, sem):
        i = pl.program_id(0)
        e = te_ref[i]
        slot = tl_ref[i]
        first = (i == 0) | (e != te_ref[jnp.maximum(i - 1, 0)])

        def copies(expert, s):
            return (pltpu.make_async_copy(wi_hbm.at[layer, expert], wi_f.at[s], sem.at[0, s]),
                    pltpu.make_async_copy(wo_hbm.at[layer, expert], wo_f.at[s], sem.at[1, s]))

        @pl.when(i == 0)
        def _():
            for cp in copies(e, slot):
                cp.start()

        @pl.when(first)
        def _():
            for cp in copies(e, slot):
                cp.wait()
            nxt = tn_ref[i]

            @pl.when(nxt >= 0)
            def _():
                for cp in copies(nxt, 1 - slot):
                    cp.start()

            wi_bf[...] = wi_f[slot].astype(bf16)
            wo_bf[...] = wo_f[slot].astype(bf16)

        @pl.when(tu_ref[i] == 1)
        def _():
            h = _dot(jnp.concatenate(_unpack2(x_ref[...]), axis=1), wi_bf[...])
            act = h[:, :DE] * _sigmoid(h[:, :DE]) * h[:, DE:]
            yv = _dot(act.astype(bf16), wo_bf[...])
            y_ref[...] = _pack2(yv[:, :D // 2], yv[:, D // 2:])

        @pl.when(tu_ref[i] == 0)
        def _():
            y_ref[...] = jnp.zeros_like(y_ref)

    grid_spec = pltpu.PrefetchScalarGridSpec(
        num_scalar_prefetch=5,
        grid=(n_tiles,),
        in_specs=[pl.BlockSpec((TE, D // 2), lambda i, te, ts, tu, tn, tl: (ts[i], 0)),
                  pl.BlockSpec(memory_space=pl.ANY),
                  pl.BlockSpec(memory_space=pl.ANY)],
        out_specs=pl.BlockSpec((TE, D // 2), lambda i, te, ts, tu, tn, tl: (i, 0)),
        scratch_shapes=[pltpu.VMEM((2, D, 2 * DE), f32), pltpu.VMEM((2, DE, D), f32),
                        pltpu.VMEM((D, 2 * DE), bf16), pltpu.VMEM((DE, D), bf16),
                        pltpu.SemaphoreType.DMA((2, 2))],
    )
    return pl.pallas_call(
        kern,
        grid_spec=grid_spec,
        out_shape=jax.ShapeDtypeStruct((n_rows, D // 2), jnp.uint32),
        compiler_params=_cparams(("arbitrary",)),
        name="moe_experts",
    )(tile_e, tile_src, tile_used, tile_next, tile_slot, xs, w_exp_in, w_exp_out)


def _combine_call(dest3, y, gt, ys, modr, layer, ln_g, ln_b, cfg, split_out=False):
    N, D, tm = cfg.N, cfg.D, cfg.TM
    alpha = cfg.alpha
    spt = cfg.seg_len // tm
    n_tiles = N // tm
    n_out = 2 if split_out else 1
    W = D // 2

    def kern(d_ref, dn_ref, y_ref, gt_ref, g2_ref, lg_ref, lb_ref, ys_ref, *rest):
        o_refs, (r0, r1, sem) = rest[:n_out], rest[n_out:]
        bufs = (r0, r1)
        i = pl.program_id(0)
        slot = i % 2

        def issue_tile(dref, s):
            def issue(g, c):
                for j in range(8):
                    for k in range(2):
                        pltpu.make_async_copy(ys_ref.at[pl.ds(dref[0, k, g * 8 + j], 1), :],
                                              bufs[k].at[s, g, pl.ds(j, 1), :], sem.at[s]).start()
                return c

            lax.fori_loop(0, tm // 8, issue, 0)

        @pl.when(i == 0)
        def _():
            issue_tile(d_ref, slot)

        @pl.when(i + 1 < n_tiles)
        def _():
            issue_tile(dn_ref, 1 - slot)

        for k in range(2):
            pltpu.make_async_copy(ys_ref.at[pl.ds(0, tm), :], ys_ref.at[pl.ds(0, tm), :], sem.at[slot]).wait()
        gt_t = gt_ref[...]
        hi0, lo0 = _unpack2(r0[slot].reshape(tm, W))
        hi1, lo1 = _unpack2(r1[slot].reshape(tm, W))
        g0, g1 = gt_t[:, 0:1], gt_t[:, 1:2]
        f = jnp.concatenate([g0 * hi0.astype(f32) + g1 * hi1.astype(f32),
                             g0 * lo0.astype(f32) + g1 * lo1.astype(f32)], axis=1)
        z = alpha * y_ref[...] + g2_ref[0] * f
        res = _ln(z) * lg_ref[...] + lb_ref[...]
        if split_out:
            @pl.when(i < spt)
            def _():
                o_refs[0][...] = res

            @pl.when(i >= spt)
            def _():
                o_refs[1][...] = res
        else:
            o_refs[0][...] = res

    if split_out:
        out_specs = _rows_specs(2, tm, cfg, D)
        out_shape = [jax.ShapeDtypeStruct((cfg.seg_len, D), f32),
                     jax.ShapeDtypeStruct((N - cfg.seg_len, D), f32)]
    else:
        out_specs = pl.BlockSpec((tm, D), lambda i: (i, 0))
        out_shape = jax.ShapeDtypeStruct((N, D), f32)
    return pl.pallas_call(
        kern,
        grid=(n_tiles,),
        in_specs=[pl.BlockSpec((1, 2, tm), lambda i: (i, 0, 0), memory_space=pltpu.SMEM),
                  pl.BlockSpec((1, 2, tm), lambda i: (jnp.minimum(i + 1, n_tiles - 1), 0, 0),
                               memory_space=pltpu.SMEM),
                  pl.BlockSpec((tm, D), lambda i: (i, 0)),
                  pl.BlockSpec((tm, 128), lambda i: (i, 0)),
                  _mod_spec(cfg, layer, 5, tm),
                  pl.BlockSpec((1, D), lambda i: (0, 0)),
                  pl.BlockSpec((1, D), lambda i: (0, 0)),
                  pl.BlockSpec(memory_space=pl.ANY)],
        out_specs=out_specs,
        out_shape=out_shape,
        scratch_shapes=[pltpu.VMEM((2, tm // 8, 8, W), jnp.uint32), pltpu.VMEM((2, tm // 8, 8, W), jnp.uint32),
                        pltpu.SemaphoreType.DMA((2,))],
        compiler_params=_cparams(("arbitrary",)),
        name="moe_combine",
    )(dest3, dest3, y, gt, modr, ln_g.reshape(1, D), ln_b.reshape(1, D), ys)


def _moe_layer(y, xs_buf, modr, layer, wr2, b_router, tri, w_exp_in, w_exp_out, ln_g, ln_b, cfg, split_out):
    N, D, E, TE, tm = cfg.N, cfg.D, cfg.E, cfg.TE, cfg.TM
    u, e2, gt, cnt = _router_call(y, modr, layer, wr2, b_router, cfg)
    counts = cnt[:, 0].astype(i32)
    padded = (counts + TE - 1) // TE * TE
    pend = jnp.cumsum(padded)
    pstart = pend - padded
    n_rows = 2 * N + E * TE
    n_tiles = n_rows // TE
    tile_start = jnp.arange(n_tiles, dtype=i32) * TE
    n_used = pend[-1] // TE
    tile_src = jnp.minimum(jnp.arange(n_tiles, dtype=i32), n_used - 1)
    tile_e = jnp.minimum(jnp.sum((pend[None, :] <= (tile_src * TE)[:, None]).astype(i32), axis=1), E - 1)
    tile_used = (tile_start < pend[-1]).astype(i32)
    eids = jnp.arange(E, dtype=i32)
    nonempty = counts > 0
    later = (eids[None, :] > eids[:, None]) & nonempty[None, :]
    next_e = jnp.min(jnp.where(later, eids[None, :], E), axis=1)
    next_e = jnp.where(next_e >= E, -1, next_e)
    run_idx = jnp.cumsum(nonempty.astype(i32)) - 1
    pick = tile_e[:, None] == eids[None, :]
    tile_next = jnp.sum(jnp.where(pick, next_e[None, :], 0), axis=1)
    tile_slot = jnp.sum(jnp.where(pick, run_idx[None, :], 0), axis=1) % 2

    dest = _rank_call(e2.reshape(1, 2 * N), pstart.astype(f32).reshape(E, 1), tri, cfg)
    dest3 = dest.reshape(2, N // tm, tm).transpose(1, 0, 2)
    xs = _dispatch_call(dest3, u, xs_buf, cfg)
    ys = _expert_call(tile_e, tile_src, tile_used, tile_next, tile_slot, xs, w_exp_in, w_exp_out, layer, cfg)
    return _combine_call(dest3, y, gt, ys, modr, layer, ln_g, ln_b, cfg, split_out), xs


def _dft_tables(cfg):
    D, R, L = cfg.D, cfg.R, cfg.SEQ
    gd = D // cfg.F_GROUPS

    def cs(n, scale):
        k = np.arange(n)
        ang = 2.0 * np.pi * ((k[:, None] * k[None, :]) % n) / n
        return np.cos(ang) * scale, np.sin(ang) * scale

    cc, sc = cs(gd, 1.0 / math.sqrt(gd))
    chan = jnp.asarray(np.concatenate([cc, -sc], axis=1), bf16)
    cl, sl = cs(L, 1.0 / math.sqrt(L))
    cr, sr = cs(R, 1.0 / math.sqrt(R))
    J, KB = cfg.FFT_J, cfg.FFT_KB
    f2 = np.concatenate([cr, -sr], axis=0)
    k1m = jnp.asarray(np.kron(f2, np.eye(J)), bf16)
    eye = np.eye(KB)
    k2c = jnp.asarray(np.einsum("kn,ab->kabn", cr, eye).reshape(R * KB, KB * R), bf16)
    k2s = jnp.asarray(np.einsum("kn,ab->kabn", sr, eye).reshape(R * KB, KB * R), bf16)
    k1 = np.arange(R)
    ang = 2.0 * np.pi * (k1[:, None] * k1[None, :]) / (R * R)
    ang = ang.reshape(R, R // J, J).transpose(1, 0, 2).reshape(R // J, R * J)
    tw_re = jnp.asarray(np.repeat(np.cos(ang)[:, :, None], 128, axis=2), f32)
    tw_im = jnp.asarray(np.repeat(-np.sin(ang)[:, :, None], 128, axis=2), f32)
    return dict(chan=chan, c_seq=jnp.asarray(cl, bf16), s_seq=jnp.asarray(sl, bf16), k1m=k1m,
                k2c=k2c, k2s=k2s, tw_re=tw_re, tw_im=tw_im)


def _rot_cols(w):
    shp = w.shape
    w2 = w.reshape(shp[:-1] + (shp[-1] // 2, 2))
    return jnp.stack([-w2[..., 1], w2[..., 0]], axis=-1).reshape(shp)


def _rope_tables(cfg):
    L, RP = cfg.seg_len, cfg.ROPE
    pairs = RP // 4
    rows = L // cfg.GRID_W
    t_row = jnp.repeat(jnp.arange(rows, dtype=f32), cfg.GRID_W)
    t_col = jnp.tile(jnp.arange(cfg.GRID_W, dtype=f32), rows)
    inv = ROPE_BASE ** (-jnp.arange(pairs, dtype=f32) / pairs)
    ang = jnp.concatenate([t_row[:, None] * inv, t_col[:, None] * inv], -1)
    cos = jnp.repeat(jnp.cos(ang), 2, axis=-1)
    sin = jnp.repeat(jnp.sin(ang), 2, axis=-1)
    cos_t = jnp.concatenate([jnp.ones((L, RP), f32), jnp.tile(cos, (cfg.DEC_BATCH, 1))], 0)
    sin_t = jnp.concatenate([jnp.zeros((L, RP), f32), jnp.tile(sin, (cfg.DEC_BATCH, 1))], 0)
    return cos_t, sin_t


def _mla_weights(w_a, w_uq, w_ukv, cfg):
    D, H, QL, KVL, NP, RP, VD = cfg.D, cfg.H, cfg.QL, cfg.KVL, cfg.NOPE, cfg.ROPE, cfg.VD
    kr_w = w_a[:, QL + KVL:]
    z = jnp.zeros((D, 128 - RP), f32)
    w_a2 = jnp.concatenate([w_a[:, :QL + KVL], kr_w, z, _rot_cols(kr_w), z], axis=1).astype(bf16)
    wq = w_uq.reshape(QL, H, NP + RP)
    w_q = jnp.concatenate([wq, _rot_cols(wq[..., NP:])], axis=-1).transpose(1, 0, 2).astype(bf16)
    w_kv = w_ukv.reshape(KVL, H, NP + VD).transpose(1, 0, 2).astype(bf16)
    return w_a2, w_q, w_kv


def _forward(cfg, x_prompt, x_sample, c, cache_ckv, cache_krope, c_ctx, w_ada, b_ada, w_f_in, w_f_out,
             w_mla_a, g_mla_q, g_mla_kv, w_mla_uq, w_mla_ukv, w_mla_o, ln_mix_g, ln_mix_b,
             ln_ffn_g, ln_ffn_b, w_router, b_router, w_exp_in, w_exp_out):
    assert cfg.BATCH * cfg.SEQ == cfg.seg_len and cfg.R * cfg.R == cfg.seg_len
    assert cfg.TQ % cfg.SEQ == 0 and cfg.LK % cfg.TKV == 0
    assert cfg.R % cfg.FFT_J == 0 and cfg.R % cfg.FFT_KB == 0
    N, D, L, E = cfg.N, cfg.D, cfg.seg_len, cfg.E
    assert cfg.DEPTH >= 2
    y = (x_prompt.reshape(L, D), x_sample.reshape(cfg.DEC_BATCH * L, D))

    cvec = jnp.concatenate([c_ctx[None], c, jnp.zeros((MOD_ROWS - cfg.n_seg, D), f32)], axis=0)
    mod = _ada_call(cvec, w_ada, b_ada, cfg)
    modr = mod.reshape(cfg.DEPTH * MOD_ROWS * 6, 1, D)

    tabs = _dft_tables(cfg)
    cos_t, sin_t = _rope_tables(cfg)
    w_hi = w_router.astype(bf16)
    w_lo = (w_router - w_hi.astype(f32)).astype(bf16)
    wr2 = jnp.concatenate([jnp.concatenate([w_hi, w_lo], 1),
                           jnp.concatenate([w_hi, jnp.zeros_like(w_lo)], 1)], 0)
    rc = cfg.RC
    tri = jnp.asarray(np.triu(np.ones((rc, rc), np.float32)), bf16)

    ckv_out, kr_out = [], []
    xs_buf = jnp.zeros((2 * N + E * cfg.TE, D // 2), jnp.uint32)
    for l in range(cfg.DEPTH):
        j = l // 2
        if l % 2 == 0:
            yt = y if isinstance(y, tuple) else (y,)
            a = _fourier_in_call(yt, modr, l, w_f_in[j].astype(bf16), tabs["chan"], cfg)
            f_p = _prompt_dft_call(a, tabs["c_seq"], tabs["s_seq"], cfg)
            t = _fft_stage1_call(a, tabs["k1m"], tabs["tw_re"], tabs["tw_im"], cfg)
            f_s = _fft_stage2_call(t, tabs["k2c"], tabs["k2s"], cfg)
            y = _post_call((f_p, f_s), w_f_out[j].astype(bf16), yt, modr, l, 2, ln_mix_g[l], ln_mix_b[l], cfg)
        else:
            w_a2, w_q, w_kv = _mla_weights(w_mla_a[j], w_mla_uq[j], w_mla_ukv[j], cfg)
            cq, ckv, kr, krr = _mla_a_call(y, modr, l, w_a2, g_mla_q[j], g_mla_kv[j], cos_t, sin_t, cfg)
            ckv_out.append(ckv[:L].reshape(cfg.BATCH, cfg.SEQ, cfg.KVL))
            kr_out.append(kr[:L].reshape(cfg.BATCH, cfg.SEQ, cfg.ROPE))
            q = _q_call(cq, w_q, cos_t, sin_t, cfg)
            pad = cfg.PAST
            ckv_all = jnp.concatenate([
                jnp.concatenate([ckv[:L], jnp.zeros((pad, cfg.KVL), f32)], 0)[None],
                jnp.concatenate([cache_ckv[:, j], ckv[L:].reshape(cfg.DEC_BATCH, L, cfg.KVL)], 1)], 0)
            kr_all = jnp.concatenate([
                jnp.concatenate([krr[:L], jnp.zeros((pad, cfg.ROPE), f32)], 0)[None],
                jnp.concatenate([cache_krope[:, j], krr[L:].reshape(cfg.DEC_BATCH, L, cfg.ROPE)], 1)], 0)
            k, v = _kv_call(ckv_all, kr_all, w_kv, cfg)
            o = _attention_call(q, k, v, cfg)
            y = _post_call((o,), w_mla_o[j].astype(bf16), (y,), modr, l, 2, ln_mix_g[l], ln_mix_b[l], cfg)
        y, xs_buf = _moe_layer(y, xs_buf, modr, l, wr2, b_router, tri, w_exp_in, w_exp_out, ln_ffn_g[l],
                               ln_ffn_b[l], cfg, split_out=(l == cfg.DEPTH - 1))

    y_p = y[0].reshape(cfg.BATCH, cfg.SEQ, D)
    y_s = y[1].reshape(cfg.DEC_BATCH, L, D)
    return y_p, y_s, jnp.stack(ckv_out, axis=1), jnp.stack(kr_out, axis=1)


def kernel(x_prompt, x_sample, c, cache_ckv, cache_krope, c_ctx, w_ada, b_ada, w_f_in, w_f_out, w_mla_a, g_mla_q, g_mla_kv, w_mla_uq, w_mla_ukv, w_mla_o, ln_mix_g, ln_mix_b, ln_ffn_g, ln_ffn_b, w_router, b_router, w_exp_in, w_exp_out):
    return _forward(Cfg(), x_prompt, x_sample, c, cache_ckv, cache_krope, c_ctx, w_ada, b_ada, w_f_in,
                    w_f_out, w_mla_a, g_mla_q, g_mla_kv, w_mla_uq, w_mla_ukv, w_mla_o, ln_mix_g,
                    ln_mix_b, ln_ffn_g, ln_ffn_b, w_router, b_router, w_exp_in, w_exp_out)
```

```python
import functools
import math
from typing import NamedTuple

import numpy as np
import jax
import jax.numpy as jnp
from jax import lax
from jax.experimental import pallas as pl
from jax.experimental.pallas import tpu as pltpu

f32 = jnp.float32
bf16 = jnp.bfloat16
i32 = jnp.int32

LN_EPS = 1e-5
RMS_EPS = 1e-6
ROPE_BASE = 10000.0
ROUTED_SCALE = 2.5
LOG2E = 1.4426950408889634
NEG_BIG = -1e30
MOD_ROWS = 16
VMEM_LIMIT_V7X = 56 * 1024 * 1024


class Cfg(NamedTuple):
    D: int = 2048
    DEPTH: int = 4
    BATCH: int = 16
    SEQ: int = 256
    DEC_BATCH: int = 8
    DEC_SEQ: int = 4096
    PAST: int = 256
    GRID_W: int = 64
    F_GROUPS: int = 4
    H: int = 16
    QL: int = 512
    KVL: int = 512
    NOPE: int = 128
    ROPE: int = 64
    VD: int = 128
    E: int = 64
    NG: int = 8
    DE: int = 512
    TM: int = 512
    TN_ADA: int = 1024
    FFT_J: int = 8
    FFT_KB: int = 8
    TKV: int = 256
    TQ: int = 512
    HB: int = 2
    RC: int = 1024
    TE: int = 256

    @property
    def seg_len(self):
        return self.DEC_SEQ

    @property
    def n_seg(self):
        return 1 + self.DEC_BATCH

    @property
    def N(self):
        return self.n_seg * self.seg_len

    @property
    def R(self):
        return int(round(math.sqrt(self.seg_len)))

    @property
    def LK(self):
        return self.PAST + self.seg_len

    @property
    def alpha(self):
        return (2 * self.DEPTH) ** 0.25


def _cparams(sem):
    return pltpu.CompilerParams(dimension_semantics=sem, vmem_limit_bytes=VMEM_LIMIT_V7X)


def _ln(x):
    mu = jnp.mean(x, axis=-1, keepdims=True)
    xc = x - mu
    var = jnp.mean(xc * xc, axis=-1, keepdims=True)
    return xc * lax.rsqrt(var + LN_EPS)


def _sigmoid(x):
    return 1.0 / (1.0 + jnp.exp(-x))


def _dot(a, b):
    return jnp.dot(a, b, preferred_element_type=f32)


def _pack2(hi, lo):
    h = lax.bitcast_convert_type(hi.astype(bf16).astype(f32), jnp.uint32)
    l = lax.bitcast_convert_type(lo.astype(bf16).astype(f32), jnp.uint32)
    return h | (l >> 16)


def _unpack2_f32(w):
    return (lax.bitcast_convert_type(w & jnp.uint32(0xFFFF0000), f32),
            lax.bitcast_convert_type(w << 16, f32))


def _unpack2(w):
    hi, lo = _unpack2_f32(w)
    return hi.astype(bf16), lo.astype(bf16)


def _mod_spec(cfg, layer, which, tm):
    spt = cfg.seg_len // tm
    base = layer * MOD_ROWS * 6
    return pl.BlockSpec((1, 1, cfg.D), lambda i: (base + (i // spt) * 6 + which, 0, 0))


def _rows_specs(n_arrs, tm, cfg, width):
    spt = cfg.seg_len // tm
    if n_arrs == 2:
        return [pl.BlockSpec((tm, width), lambda i: (jnp.minimum(i, spt - 1), 0)),
                pl.BlockSpec((tm, width), lambda i: (jnp.maximum(i - spt, 0), 0))]
    return [pl.BlockSpec((tm, width), lambda i: (i, 0))]


def _rows_read(refs, spt):
    if len(refs) == 2:
        return jnp.where(pl.program_id(0) < spt, refs[0][...], refs[1][...])
    return refs[0][...]


def _ada_call(cvec, w_ada, b_ada, cfg):
    D, n6, tn = cfg.D, 6 * cfg.D, cfg.TN_ADA

    def kern(c_ref, w_ref, b_ref, o_ref):
        c = c_ref[...]
        s = (c * _sigmoid(c)).astype(bf16)
        o_ref[0] = _dot(s, w_ref[0].astype(bf16)) + b_ref[0]

    return pl.pallas_call(
        kern,
        grid=(cfg.DEPTH, n6 // tn),
        in_specs=[pl.BlockSpec((MOD_ROWS, D), lambda l, j: (0, 0)),
                  pl.BlockSpec((1, D, tn), lambda l, j: (l, 0, j)),
                  pl.BlockSpec((1, 1, tn), lambda l, j: (l, 0, j))],
        out_specs=pl.BlockSpec((1, MOD_ROWS, tn), lambda l, j: (l, 0, j)),
        out_shape=jax.ShapeDtypeStruct((cfg.DEPTH, MOD_ROWS, n6), f32),
        compiler_params=_cparams(("arbitrary", "arbitrary")),
        name="ada",
    )(cvec, w_ada, b_ada.reshape(cfg.DEPTH, 1, n6))


def _fourier_in_call(y, modr, layer, w_in, cs, cfg):
    N, D, tm = cfg.N, cfg.D, cfg.TM
    gd = D // cfg.F_GROUPS
    ny = len(y)
    spt = cfg.seg_len // tm

    def kern(*refs):
        y_refs, (sh_ref, sc_ref, w_ref, cs_ref, a_ref) = refs[:ny], refs[ny:]
        u = _ln(_rows_read(y_refs, spt)) * (1.0 + sc_ref[0]) + sh_ref[0]
        h = _dot(u.astype(bf16), w_ref[...]).astype(bf16)
        for g in range(cfg.F_GROUPS):
            a = _dot(h[:, g * gd:(g + 1) * gd], cs_ref[...])
            a_ref[:, g * gd:(g + 1) * gd] = _pack2(a[:, :gd], a[:, gd:])

    return pl.pallas_call(
        kern,
        grid=(N // tm,),
        in_specs=_rows_specs(ny, tm, cfg, D) + [
            _mod_spec(cfg, layer, 0, tm), _mod_spec(cfg, layer, 1, tm),
            pl.BlockSpec((D, D), lambda i: (0, 0)),
            pl.BlockSpec((gd, 2 * gd), lambda i: (0, 0))],
        out_specs=pl.BlockSpec((tm, D), lambda i: (i, 0)),
        out_shape=jax.ShapeDtypeStruct((N, D), jnp.uint32),
        compiler_params=_cparams(("arbitrary",)),
        name="fourier_in",
    )(*y, modr, modr, w_in, cs)


def _prompt_dft_call(a, c_seq, s_seq, cfg):
    D, L = cfg.D, cfg.SEQ
    gd = D // cfg.F_GROUPS
    a3 = a.reshape(cfg.N // L, L, D)

    def kern(c_ref, s_ref, x_ref, o_ref):
        re, im = _unpack2(x_ref[0])
        o_ref[0] = _dot(c_ref[...], re) + _dot(s_ref[...], im)

    out = pl.pallas_call(
        kern,
        grid=(cfg.BATCH, cfg.F_GROUPS),
        in_specs=[pl.BlockSpec((L, L), lambda s, g: (0, 0)),
                  pl.BlockSpec((L, L), lambda s, g: (0, 0)),
                  pl.BlockSpec((1, L, gd), lambda s, g: (s, 0, g))],
        out_specs=pl.BlockSpec((1, L, gd), lambda s, g: (s, 0, g)),
        out_shape=jax.ShapeDtypeStruct((cfg.BATCH, L, D), f32),
        compiler_params=_cparams(("arbitrary", "arbitrary")),
        name="prompt_dft",
    )(c_seq, s_seq, a3)
    return out.reshape(cfg.seg_len, D)


def _fft_stage1_call(a, k1m, tw_re, tw_im, cfg):
    D, R, J = cfg.D, cfg.R, cfg.FFT_J
    gd = D // cfg.F_GROUPS
    rj = R * J
    a4 = a.reshape(cfg.n_seg, R, R, D)

    def kern(km_ref, twr_ref, twi_ref, x_ref, o_ref):
        x = jnp.concatenate(_unpack2(x_ref[0].reshape(rj, gd)), axis=1)
        pq = _dot(km_ref[...], x)
        p, q = pq[:rj], pq[rj:]
        t_re = p[:, :gd] - q[:, gd:]
        t_im = q[:, :gd] + p[:, gd:]
        twr = jnp.tile(twr_ref[0], (1, gd // 128))
        twi = jnp.tile(twi_ref[0], (1, gd // 128))
        o_ref[0] = _pack2(t_re * twr - t_im * twi, t_re * twi + t_im * twr).reshape(R, J, gd)

    return pl.pallas_call(
        kern,
        grid=(cfg.DEC_BATCH, R // J, cfg.F_GROUPS),
        in_specs=[pl.BlockSpec((2 * rj, rj), lambda b, j, g: (0, 0)),
                  pl.BlockSpec((1, rj, 128), lambda b, j, g: (j, 0, 0)),
                  pl.BlockSpec((1, rj, 128), lambda b, j, g: (j, 0, 0)),
                  pl.BlockSpec((1, R, J, gd), lambda b, j, g: (b + 1, 0, j, g))],
        out_specs=pl.BlockSpec((1, R, J, gd), lambda b, j, g: (b, 0, j, g)),
        out_shape=jax.ShapeDtypeStruct((cfg.DEC_BATCH, R, R, D), jnp.uint32),
        compiler_params=_cparams(("arbitrary", "arbitrary", "arbitrary")),
        name="fft_stage1",
    )(k1m, tw_re, tw_im, a4)


def _fft_stage2_call(t4, k2c, k2s, cfg):
    D, R, KB = cfg.D, cfg.R, cfg.FFT_KB
    gd = D // cfg.F_GROUPS
    t3 = t4.reshape(cfg.DEC_BATCH, R * R, D)

    def kern(kc_ref, ks_ref, x_ref, o_ref):
        re, im = _unpack2(x_ref[0])
        yv = _dot(kc_ref[...], re) + _dot(ks_ref[...], im)
        o_ref[0] = yv.reshape(R, KB, gd)

    out = pl.pallas_call(
        kern,
        grid=(cfg.DEC_BATCH, R // KB, cfg.F_GROUPS),
        in_specs=[pl.BlockSpec((R * KB, KB * R), lambda b, k, g: (0, 0)),
                  pl.BlockSpec((R * KB, KB * R), lambda b, k, g: (0, 0)),
                  pl.BlockSpec((1, KB * R, gd), lambda b, k, g: (b, k, g))],
        out_specs=pl.BlockSpec((1, R, KB, gd), lambda b, k, g: (b, 0, k, g)),
        out_shape=jax.ShapeDtypeStruct((cfg.DEC_BATCH, R, R, D), f32),
        compiler_params=_cparams(("arbitrary", "arbitrary", "arbitrary")),
        name="fft_stage2",
    )(k2c, k2s, t3)
    return out.reshape(cfg.DEC_BATCH * cfg.seg_len, D)


def _post_call(fs, w_out, y, modr, layer, which, ln_g, ln_b, cfg):
    N, D, tm = cfg.N, cfg.D, cfg.TM
    spt = cfg.seg_len // tm
    nf, ny = len(fs), len(y)
    alpha = cfg.alpha

    def kern(*refs):
        f_refs, w_ref, y_refs = refs[:nf], refs[nf], refs[nf + 1:nf + 1 + ny]
        g_ref, lg_ref, lb_ref, o_ref = refs[nf + 1 + ny:]
        m = _dot(_rows_read(f_refs, spt).astype(bf16), w_ref[...])
        z = alpha * _rows_read(y_refs, spt) + g_ref[0] * m
        o_ref[...] = _ln(z) * lg_ref[...] + lb_ref[...]

    return pl.pallas_call(
        kern,
        grid=(N // tm,),
        in_specs=_rows_specs(nf, tm, cfg, D) + [pl.BlockSpec((D, D), lambda i: (0, 0))]
        + _rows_specs(ny, tm, cfg, D) + [_mod_spec(cfg, layer, which, tm),
                                         pl.BlockSpec((1, D), lambda i: (0, 0)),
                                         pl.BlockSpec((1, D), lambda i: (0, 0))],
        out_specs=pl.BlockSpec((tm, D), lambda i: (i, 0)),
        out_shape=jax.ShapeDtypeStruct((N, D), f32),
        compiler_params=_cparams(("arbitrary",)),
        name="mixer_post",
    )(*fs, w_out, *y, modr, ln_g.reshape(1, D), ln_b.reshape(1, D))


def _mla_a_call(y, modr, layer, w_a2, g_q, g_kv, cos_t, sin_t, cfg):
    N, D, tm = cfg.N, cfg.D, cfg.TM
    QL, KVL, RP = cfg.QL, cfg.KVL, cfg.ROPE
    wa = w_a2.shape[1]
    o_kr = QL + KVL
    o_rot = o_kr + 128

    def rms(x, g):
        return x * lax.rsqrt(jnp.mean(x * x, axis=-1, keepdims=True) + RMS_EPS) * g

    def kern(y_ref, sh_ref, sc_ref, w_ref, gq_ref, gkv_ref, cos_ref, sin_ref,
             cq_ref, ckv_ref, kr_ref, krr_ref):
        u = _ln(y_ref[...]) * (1.0 + sc_ref[0]) + sh_ref[0]
        a = _dot(u.astype(bf16), w_ref[...])
        cq_ref[...] = rms(a[:, :QL], gq_ref[...]).astype(bf16)
        ckv_ref[...] = rms(a[:, QL:QL + KVL], gkv_ref[...])
        kr = a[:, o_kr:o_kr + RP]
        kr_ref[...] = kr
        krr_ref[...] = kr * cos_ref[...] + a[:, o_rot:o_rot + RP] * sin_ref[...]

    return pl.pallas_call(
        kern,
        grid=(N // tm,),
        in_specs=[pl.BlockSpec((tm, D), lambda i: (i, 0)),
                  _mod_spec(cfg, layer, 0, tm), _mod_spec(cfg, layer, 1, tm),
                  pl.BlockSpec((D, wa), lambda i: (0, 0)),
                  pl.BlockSpec((1, QL), lambda i: (0, 0)),
                  pl.BlockSpec((1, KVL), lambda i: (0, 0)),
                  pl.BlockSpec((tm, RP), lambda i: (i, 0)),
                  pl.BlockSpec((tm, RP), lambda i: (i, 0))],
        out_specs=[pl.BlockSpec((tm, QL), lambda i: (i, 0)),
                   pl.BlockSpec((tm, KVL), lambda i: (i, 0)),
                   pl.BlockSpec((tm, RP), lambda i: (i, 0)),
                   pl.BlockSpec((tm, RP), lambda i: (i, 0))],
        out_shape=[jax.ShapeDtypeStruct((N, QL), bf16),
                   jax.ShapeDtypeStruct((N, KVL), f32),
                   jax.ShapeDtypeStruct((N, RP), f32),
                   jax.ShapeDtypeStruct((N, RP), f32)],
        compiler_params=_cparams(("arbitrary",)),
        name="mla_down",
    )(y, modr, modr, w_a2, g_q.reshape(1, QL), g_kv.reshape(1, KVL), cos_t, sin_t)


def _q_call(cq, w_q, cos_t, sin_t, cfg):
    N, tm, H = cfg.N, cfg.TM, cfg.H
    QL, NP, RP = cfg.QL, cfg.NOPE, cfg.ROPE
    dk = NP + RP
    wq = NP + 2 * RP
    qscale = LOG2E / math.sqrt(dk)

    def kern(cq_ref, w_ref, cos_ref, sin_ref, q_ref):
        cq_t = cq_ref[...]
        cos, sin = cos_ref[...], sin_ref[...]
        for h in range(H):
            t = _dot(cq_t, w_ref[h])
            rope = t[:, NP:NP + RP] * cos + t[:, NP + RP:] * sin
            q_ref[h] = (jnp.concatenate([t[:, :NP], rope], axis=-1) * qscale).astype(bf16)

    return pl.pallas_call(
        kern,
        grid=(N // tm,),
        in_specs=[pl.BlockSpec((tm, QL), lambda i: (i, 0)),
                  pl.BlockSpec((H, QL, wq), lambda i: (0, 0, 0)),
                  pl.BlockSpec((tm, RP), lambda i: (i, 0)),
                  pl.BlockSpec((tm, RP), lambda i: (i, 0))],
        out_specs=pl.BlockSpec((H, tm, dk), lambda i: (0, i, 0)),
        out_shape=jax.ShapeDtypeStruct((H, N, dk), bf16),
        compiler_params=_cparams(("arbitrary",)),
        name="mla_q",
    )(cq, w_q, cos_t, sin_t)


def _kv_call(ckv_all, kr_all, w_kv, cfg):
    H, KVL, NP, RP, VD = cfg.H, cfg.KVL, cfg.NOPE, cfg.ROPE, cfg.VD
    LK, tkv, ns = cfg.LK, cfg.TKV, cfg.n_seg
    dk = NP + RP

    def kern(c_ref, kr_ref, w_ref, k_ref, v_ref):
        c = c_ref[0].astype(bf16)
        kr = kr_ref[0]
        for h in range(H):
            t = _dot(c, w_ref[h])
            k_ref[h, 0] = jnp.concatenate([t[:, :NP], kr], axis=-1).astype(bf16)
            v_ref[h, 0] = jnp.concatenate([t[:, NP:], jnp.ones((tkv, 128), f32)], axis=-1).astype(bf16)

    return pl.pallas_call(
        kern,
        grid=(ns, LK // tkv),
        in_specs=[pl.BlockSpec((1, tkv, KVL), lambda s, i: (s, i, 0)),
                  pl.BlockSpec((1, tkv, RP), lambda s, i: (s, i, 0)),
                  pl.BlockSpec((H, KVL, NP + VD), lambda s, i: (0, 0, 0))],
        out_specs=[pl.BlockSpec((H, 1, tkv, dk), lambda s, i: (0, s, i, 0)),
                   pl.BlockSpec((H, 1, tkv, VD + 128), lambda s, i: (0, s, i, 0))],
        out_shape=[jax.ShapeDtypeStruct((H, ns, LK, dk), bf16),
                   jax.ShapeDtypeStruct((H, ns, LK, VD + 128), bf16)],
        compiler_params=_cparams(("arbitrary", "arbitrary")),
        name="mla_kv",
    )(ckv_all, kr_all, w_kv)


def _attention_call(q, k, v, cfg):
    H, NP, RP, VD = cfg.H, cfg.NOPE, cfg.ROPE, cfg.VD
    assert VD == 128
    dk = NP + RP
    LK, TQ, SEQ, HB = cfg.LK, cfg.TQ, cfg.SEQ, cfg.HB
    nq = cfg.seg_len // TQ
    sub = TQ // SEQ

    def scores(qb, kb):
        return lax.dot_general(qb, kb, (((1,), (1,)), ((), ())), preferred_element_type=f32)

    def finish(s, vb):
        p = jnp.exp2(s - jnp.max(s, axis=-1, keepdims=True)).astype(bf16)
        acc = _dot(p, vb)
        return (acc[:, :VD] / acc[:, VD:]).astype(bf16)

    def kern(q_ref, k_ref, v_ref, o_ref):
        seg = pl.program_id(0)
        qi = pl.program_id(2)

        @pl.when(seg == 0)
        def _():
            for hh in range(HB):
                for j in range(sub):
                    off = pl.multiple_of((qi * sub + j) * SEQ, SEQ)
                    s = scores(q_ref[hh, j * SEQ:(j + 1) * SEQ, :], k_ref[hh, 0, pl.ds(off, SEQ), :])
                    o_ref[j * SEQ:(j + 1) * SEQ, hh * VD:(hh + 1) * VD] = finish(
                        s, v_ref[hh, 0, pl.ds(off, SEQ), :])

        @pl.when(seg != 0)
        def _():
            ss = [scores(q_ref[hh], k_ref[hh, 0]) for hh in range(HB)]
            for hh in range(HB):
                o_ref[:, hh * VD:(hh + 1) * VD] = finish(ss[hh], v_ref[hh, 0])

    return pl.pallas_call(
        kern,
        grid=(cfg.n_seg, H // HB, nq),
        in_specs=[pl.BlockSpec((HB, TQ, dk), lambda s, h, i: (h, s * nq + i, 0)),
                  pl.BlockSpec((HB, 1, LK, dk), lambda s, h, i: (h, s, 0, 0)),
                  pl.BlockSpec((HB, 1, LK, VD + 128), lambda s, h, i: (h, s, 0, 0))],
        out_specs=pl.BlockSpec((TQ, HB * VD), lambda s, h, i: (s * nq + i, h)),
        out_shape=jax.ShapeDtypeStruct((cfg.N, H * VD), bf16),
        compiler_params=_cparams(("arbitrary", "arbitrary", "arbitrary")),
        name="mla_attention",
    )(q, k, v)


def _router_call(y, modr, layer, wr2, b_router, cfg):
    N, D, tm, E, NG = cfg.N, cfg.D, cfg.TM, cfg.E, cfg.NG
    epg = E // NG

    def top2(x, jio):
        m1 = jnp.max(x, axis=0, keepdims=True)
        i1 = jnp.min(jnp.where(x == m1, jio, epg), axis=0, keepdims=True)
        x2 = jnp.where(jio == i1, -jnp.inf, x)
        m2 = jnp.max(x2, axis=0, keepdims=True)
        i2 = jnp.min(jnp.where(x2 == m2, jio, epg), axis=0, keepdims=True)
        return m1, i1, m2, i2

    def kern(y_ref, sh_ref, sc_ref, w_ref, b_ref, u_ref, e_ref, g_ref, cnt_ref):
        i = pl.program_id(0)
        u = _ln(y_ref[...]) * (1.0 + sc_ref[0]) + sh_ref[0]
        u_hi = u.astype(bf16)
        u_ref[...] = _pack2(u[:, :D // 2], u[:, D // 2:])
        u_lo = (u - u_hi.astype(f32)).astype(bf16)
        out = _dot(jnp.concatenate([u_hi, u_lo], axis=1), w_ref[...])
        lt = out.T
        logits = lt[:E] + lt[E:]
        scores = _sigmoid(logits)
        biased = scores + b_ref[...]
        jio = lax.broadcasted_iota(i32, (epg, tm), 0)
        best = gi = None
        for g in range(NG):
            m1, _, m2, _ = top2(biased[g * epg:(g + 1) * epg], jio)
            gs = m1 + m2
            if g == 0:
                best, gi = gs, jnp.zeros((1, tm), i32)
            else:
                better = gs > best
                best = jnp.where(better, gs, best)
                gi = jnp.where(better, g, gi)
        sel_b = biased[:epg]
        sel_s = scores[:epg]
        for g in range(1, NG):
            pick = gi == g
            sel_b = jnp.where(pick, biased[g * epg:(g + 1) * epg], sel_b)
            sel_s = jnp.where(pick, scores[g * epg:(g + 1) * epg], sel_s)
        _, i1, _, i2 = top2(sel_b, jio)
        s1 = jnp.sum(jnp.where(jio == i1, sel_s, 0.0), axis=0, keepdims=True)
        s2 = jnp.sum(jnp.where(jio == i2, sel_s, 0.0), axis=0, keepdims=True)
        den = s1 + s2
        e1 = gi * epg + i1
        e2 = gi * epg + i2
        e_ref[0:1, :] = e1
        e_ref[1:2, :] = e2
        rio = lax.broadcasted_iota(i32, (128, tm), 0)
        gates = jnp.where(rio == 0, ROUTED_SCALE * s1 / den,
                          jnp.where(rio == 1, ROUTED_SCALE * s2 / den, 0.0))
        g_ref[...] = gates.T
        eio = lax.broadcasted_iota(i32, (E, tm), 0)
        hits = jnp.where(eio == e1, 1.0, 0.0) + jnp.where(eio == e2, 1.0, 0.0)
        part = jnp.sum(hits, axis=1, keepdims=True)

        @pl.when(i == 0)
        def _():
            cnt_ref[...] = jnp.zeros_like(cnt_ref)

        cnt_ref[...] += part

    return pl.pallas_call(
        kern,
        grid=(N // tm,),
        in_specs=[pl.BlockSpec((tm, D), lambda i: (i, 0)),
                  _mod_spec(cfg, layer, 3, tm), _mod_spec(cfg, layer, 4, tm),
                  pl.BlockSpec((2 * D, 2 * E), lambda i: (0, 0)),
                  pl.BlockSpec((E, 1), lambda i: (0, 0))],
        out_specs=[pl.BlockSpec((tm, D // 2), lambda i: (i, 0)),
                   pl.BlockSpec((2, tm), lambda i: (0, i)),
                   pl.BlockSpec((tm, 128), lambda i: (i, 0)),
                   pl.BlockSpec((E, 128), lambda i: (0, 0))],
        out_shape=[jax.ShapeDtypeStruct((N, D // 2), jnp.uint32),
                   jax.ShapeDtypeStruct((2, N), i32),
                   jax.ShapeDtypeStruct((N, 128), f32),
                   jax.ShapeDtypeStruct((E, 128), f32)],
        compiler_params=_cparams(("arbitrary",)),
        name="moe_router",
    )(y, modr, modr, wr2, b_router.reshape(E, 1))


def _rank_call(e_flat, pstart, tri, cfg):
    E, C = cfg.E, cfg.RC
    n_asg = e_flat.shape[1]

    def kern(e_ref, ps_ref, tri_ref, d_ref, base_ref):
        @pl.when(pl.program_id(0) == 0)
        def _():
            base_ref[...] = jnp.zeros_like(base_ref)

        eio = lax.broadcasted_iota(i32, (E, C), 0)
        oh = eio == e_ref[...]
        incl = _dot(jnp.where(oh, 1.0, 0.0).astype(bf16), tri_ref[...])
        base = base_ref[:, 0:1]
        pos = ps_ref[...] + base + incl - 1.0
        d_ref[...] = jnp.sum(jnp.where(oh, pos, 0.0), axis=0, keepdims=True).astype(i32)
        base_ref[...] += incl[:, C - 1:C]

    return pl.pallas_call(
        kern,
        grid=(n_asg // C,),
        in_specs=[pl.BlockSpec((1, C), lambda i: (0, i)),
                  pl.BlockSpec((E, 1), lambda i: (0, 0)),
                  pl.BlockSpec((C, C), lambda i: (0, 0))],
        out_specs=pl.BlockSpec((1, C), lambda i: (0, i)),
        out_shape=jax.ShapeDtypeStruct((1, n_asg), i32),
        scratch_shapes=[pltpu.VMEM((E, 128), f32)],
        compiler_params=_cparams(("arbitrary",)),
        name="moe_rank",
    )(e_flat, pstart, tri)


def _dispatch_call(dest3, u, xs0, cfg):
    N, D, tm = cfg.N, cfg.D, cfg.TM

    def kern(d_ref, u_ref, xs_in_ref, xs_ref, sem):
        del xs_in_ref

        def issue(g, c):
            for j in range(8):
                for k in range(2):
                    pltpu.make_async_copy(u_ref.at[g, pl.ds(j, 1), :],
                                          xs_ref.at[pl.ds(d_ref[0, g, k * 8 + j], 1), :], sem).start()
            return c

        lax.fori_loop(0, tm // 8, issue, 0)
        for k in range(2):
            pltpu.make_async_copy(xs_ref.at[pl.ds(0, tm), :], xs_ref.at[pl.ds(0, tm), :], sem).wait()

    return pl.pallas_call(
        kern,
        grid=(N // tm,),
        in_specs=[pl.BlockSpec((1, tm // 8, 16), lambda i: (i, 0, 0), memory_space=pltpu.SMEM),
                  pl.BlockSpec((tm // 8, 8, D // 2), lambda i: (i, 0, 0)),
                  pl.BlockSpec(memory_space=pl.ANY)],
        out_specs=pl.BlockSpec(memory_space=pl.ANY),
        out_shape=jax.ShapeDtypeStruct(xs0.shape, xs0.dtype),
        scratch_shapes=[pltpu.SemaphoreType.DMA(())],
        input_output_aliases={2: 0},
        compiler_params=_cparams(("arbitrary",)),
        name="moe_dispatch",
    )(dest3, u.reshape(N // 8, 8, D // 2), xs0)


def _expert_call(tile_e, tile_src, tile_used, tile_next, tile_slot, xs, w_exp_in, w_exp_out, layer, cfg):
    D, DE, TE = cfg.D, cfg.DE, cfg.TE
    n_rows = xs.shape[0]
    n_tiles = n_rows // TE

    def kern(te_ref, ts_ref, tu_ref, tn_ref, tl_ref, x_ref, wi_hbm, wo_hbm, y_ref,
             wi_f, wo_f, wi_bf, wo_bf, sem):
        i = pl.program_id(0)
        e = te_ref[i]
        slot = tl_ref[i]
        first = (i == 0) | (e != te_ref[jnp.maximum(i - 1, 0)])

        def copies(expert, s):
            return (pltpu.make_async_copy(wi_hbm.at[layer, expert], wi_f.at[s], sem.at[0, s]),
                    pltpu.make_async_copy(wo_hbm.at[layer, expert], wo_f.at[s], sem.at[1, s]))

        @pl.when(i == 0)
        def _():
            for cp in copies(e, slot):
                cp.start()

        @pl.when(first)
        def _():
            for cp in copies(e, slot):
                cp.wait()
            nxt = tn_ref[i]

            @pl.when(nxt >= 0)
            def _():
                for cp in copies(nxt, 1 - slot):
                    cp.start()

            wi_bf[...] = wi_f[slot].astype(bf16)
            wo_bf[...] = wo_f[slot].astype(bf16)

        @pl.when(tu_ref[i] == 1)
        def _():
            h = _dot(jnp.concatenate(_unpack2(x_ref[...]), axis=1), wi_bf[...])
            act = h[:, :DE] * _sigmoid(h[:, :DE]) * h[:, DE:]
            yv = _dot(act.astype(bf16), wo_bf[...])
            y_ref[...] = _pack2(yv[:, :D // 2], yv[:, D // 2:])

        @pl.when(tu_ref[i] == 0)
        def _():
            y_ref[...] = jnp.zeros_like(y_ref)

    grid_spec = pltpu.PrefetchScalarGridSpec(
        num_scalar_prefetch=5,
        grid=(n_tiles,),
        in_specs=[pl.BlockSpec((TE, D // 2), lambda i, te, ts, tu, tn, tl: (ts[i], 0)),
                  pl.BlockSpec(memory_space=pl.ANY),
                  pl.BlockSpec(memory_space=pl.ANY)],
        out_specs=pl.BlockSpec((TE, D // 2), lambda i, te, ts, tu, tn, tl: (i, 0)),
        scratch_shapes=[pltpu.VMEM((2, D, 2 * DE), f32), pltpu.VMEM((2, DE, D), f32),
                        pltpu.VMEM((D, 2 * DE), bf16), pltpu.VMEM((DE, D), bf16),
                        pltpu.SemaphoreType.DMA((2, 2))],
    )
    return pl.pallas_call(
        kern,
        grid_spec=grid_spec,
        out_shape=jax.ShapeDtypeStruct((n_rows, D // 2), jnp.uint32),
        compiler_params=_cparams(("arbitrary",)),
        name="moe_experts",
    )(tile_e, tile_src, tile_used, tile_next, tile_slot, xs, w_exp_in, w_exp_out)


def _combine_call(dest3, y, gt, ys, modr, layer, ln_g, ln_b, cfg, split_out=False):
    N, D, tm = cfg.N, cfg.D, cfg.TM
    alpha = cfg.alpha
    spt = cfg.seg_len // tm
    n_tiles = N // tm
    n_out = 2 if split_out else 1
    W = D // 2

    def kern(d_ref, dn_ref, y_ref, gt_ref, g2_ref, lg_ref, lb_ref, ys_ref, *rest):
        o_refs, (r0, r1, sem) = rest[:n_out], rest[n_out:]
        bufs = (r0, r1)
        i = pl.program_id(0)
        slot = i % 2

        def issue_tile(dref, s):
            def issue(g, c):
                for j in range(8):
                    for k in range(2):
                        pltpu.make_async_copy(ys_ref.at[pl.ds(dref[0, g, k * 8 + j], 1), :],
                                              bufs[k].at[s, g, pl.ds(j, 1), :], sem.at[s]).start()
                return c

            lax.fori_loop(0, tm // 8, issue, 0)

        @pl.when(i == 0)
        def _():
            issue_tile(d_ref, slot)

        @pl.when(i + 1 < n_tiles)
        def _():
            issue_tile(dn_ref, 1 - slot)

        for k in range(2):
            pltpu.make_async_copy(ys_ref.at[pl.ds(0, tm), :], ys_ref.at[pl.ds(0, tm), :], sem.at[slot]).wait()
        gt_t = gt_ref[...]
        hi0, lo0 = _unpack2_f32(r0[slot].reshape(tm, W))
        hi1, lo1 = _unpack2_f32(r1[slot].reshape(tm, W))
        g0, g1 = gt_t[:, 0:1], gt_t[:, 1:2]
        f = jnp.concatenate([g0 * hi0 + g1 * hi1, g0 * lo0 + g1 * lo1], axis=1)
        z = alpha * y_ref[...] + g2_ref[0] * f
        res = _ln(z) * lg_ref[...] + lb_ref[...]
        if split_out:
            @pl.when(i < spt)
            def _():
                o_refs[0][...] = res

            @pl.when(i >= spt)
            def _():
                o_refs[1][...] = res
        else:
            o_refs[0][...] = res

    if split_out:
        out_specs = _rows_specs(2, tm, cfg, D)
        out_shape = [jax.ShapeDtypeStruct((cfg.seg_len, D), f32),
                     jax.ShapeDtypeStruct((N - cfg.seg_len, D), f32)]
    else:
        out_specs = pl.BlockSpec((tm, D), lambda i: (i, 0))
        out_shape = jax.ShapeDtypeStruct((N, D), f32)
    return pl.pallas_call(
        kern,
        grid=(n_tiles,),
        in_specs=[pl.BlockSpec((1, tm // 8, 16), lambda i: (i, 0, 0), memory_space=pltpu.SMEM),
                  pl.BlockSpec((1, tm // 8, 16), lambda i: (jnp.minimum(i + 1, n_tiles - 1), 0, 0),
                               memory_space=pltpu.SMEM),
                  pl.BlockSpec((tm, D), lambda i: (i, 0)),
                  pl.BlockSpec((tm, 128), lambda i: (i, 0)),
                  _mod_spec(cfg, layer, 5, tm),
                  pl.BlockSpec((1, D), lambda i: (0, 0)),
                  pl.BlockSpec((1, D), lambda i: (0, 0)),
                  pl.BlockSpec(memory_space=pl.ANY)],
        out_specs=out_specs,
        out_shape=out_shape,
        scratch_shapes=[pltpu.VMEM((2, tm // 8, 8, W), jnp.uint32), pltpu.VMEM((2, tm // 8, 8, W), jnp.uint32),
                        pltpu.SemaphoreType.DMA((2,))],
        compiler_params=_cparams(("arbitrary",)),
        name="moe_combine",
    )(dest3, dest3, y, gt, modr, ln_g.reshape(1, D), ln_b.reshape(1, D), ys)


def _moe_layer(y, xs_buf, modr, layer, wr2, b_router, tri, w_exp_in, w_exp_out, ln_g, ln_b, cfg, split_out):
    N, D, E, TE, tm = cfg.N, cfg.D, cfg.E, cfg.TE, cfg.TM
    u, e2, gt, cnt = _router_call(y, modr, layer, wr2, b_router, cfg)
    counts = cnt[:, 0].astype(i32)
    padded = (counts + TE - 1) // TE * TE
    pend = jnp.cumsum(padded)
    pstart = pend - padded
    n_rows = 2 * N + E * TE
    n_tiles = n_rows // TE
    tile_start = jnp.arange(n_tiles, dtype=i32) * TE
    n_used = pend[-1] // TE
    tile_src = jnp.minimum(jnp.arange(n_tiles, dtype=i32), n_used - 1)
    tile_e = jnp.minimum(jnp.sum((pend[None, :] <= (tile_src * TE)[:, None]).astype(i32), axis=1), E - 1)
    tile_used = (tile_start < pend[-1]).astype(i32)
    eids = jnp.arange(E, dtype=i32)
    nonempty = counts > 0
    later = (eids[None, :] > eids[:, None]) & nonempty[None, :]
    next_e = jnp.min(jnp.where(later, eids[None, :], E), axis=1)
    next_e = jnp.where(next_e >= E, -1, next_e)
    run_idx = jnp.cumsum(nonempty.astype(i32)) - 1
    pick = tile_e[:, None] == eids[None, :]
    tile_next = jnp.sum(jnp.where(pick, next_e[None, :], 0), axis=1)
    tile_slot = jnp.sum(jnp.where(pick, run_idx[None, :], 0), axis=1) % 2

    dest = _rank_call(e2.reshape(1, 2 * N), pstart.astype(f32).reshape(E, 1), tri, cfg)
    dest3 = dest.reshape(2, N // tm, tm // 8, 8).transpose(1, 2, 0, 3).reshape(N // tm, tm // 8, 16)
    xs = _dispatch_call(dest3, u, xs_buf, cfg)
    ys = _expert_call(tile_e, tile_src, tile_used, tile_next, tile_slot, xs, w_exp_in, w_exp_out, layer, cfg)
    return _combine_call(dest3, y, gt, ys, modr, layer, ln_g, ln_b, cfg, split_out), xs


def _dft_tables(cfg):
    D, R, L = cfg.D, cfg.R, cfg.SEQ
    gd = D // cfg.F_GROUPS

    def cs(n, scale):
        k = np.arange(n)
        ang = 2.0 * np.pi * ((k[:, None] * k[None, :]) % n) / n
        return np.cos(ang) * scale, np.sin(ang) * scale

    cc, sc = cs(gd, 1.0 / math.sqrt(gd))
    chan = jnp.asarray(np.concatenate([cc, -sc], axis=1), bf16)
    cl, sl = cs(L, 1.0 / math.sqrt(L))
    cr, sr = cs(R, 1.0 / math.sqrt(R))
    J, KB = cfg.FFT_J, cfg.FFT_KB
    f2 = np.concatenate([cr, -sr], axis=0)
    k1m = jnp.asarray(np.kron(f2, np.eye(J)), bf16)
    eye = np.eye(KB)
    k2c = jnp.asarray(np.einsum("kn,ab->kabn", cr, eye).reshape(R * KB, KB * R), bf16)
    k2s = jnp.asarray(np.einsum("kn,ab->kabn", sr, eye).reshape(R * KB, KB * R), bf16)
    k1 = np.arange(R)
    ang = 2.0 * np.pi * (k1[:, None] * k1[None, :]) / (R * R)
    ang = ang.reshape(R, R // J, J).transpose(1, 0, 2).reshape(R // J, R * J)
    tw_re = jnp.asarray(np.repeat(np.cos(ang)[:, :, None], 128, axis=2), f32)
    tw_im = jnp.asarray(np.repeat(-np.sin(ang)[:, :, None], 128, axis=2), f32)
    return dict(chan=chan, c_seq=jnp.asarray(cl, bf16), s_seq=jnp.asarray(sl, bf16), k1m=k1m,
                k2c=k2c, k2s=k2s, tw_re=tw_re, tw_im=tw_im)


def _rot_cols(w):
    shp = w.shape
    w2 = w.reshape(shp[:-1] + (shp[-1] // 2, 2))
    return jnp.stack([-w2[..., 1], w2[..., 0]], axis=-1).reshape(shp)


def _rope_tables(cfg):
    L, RP = cfg.seg_len, cfg.ROPE
    pairs = RP // 4
    rows = L // cfg.GRID_W
    t_row = jnp.repeat(jnp.arange(rows, dtype=f32), cfg.GRID_W)
    t_col = jnp.tile(jnp.arange(cfg.GRID_W, dtype=f32), rows)
    inv = ROPE_BASE ** (-jnp.arange(pairs, dtype=f32) / pairs)
    ang = jnp.concatenate([t_row[:, None] * inv, t_col[:, None] * inv], -1)
    cos = jnp.repeat(jnp.cos(ang), 2, axis=-1)
    sin = jnp.repeat(jnp.sin(ang), 2, axis=-1)
    cos_t = jnp.concatenate([jnp.ones((L, RP), f32), jnp.tile(cos, (cfg.DEC_BATCH, 1))], 0)
    sin_t = jnp.concatenate([jnp.zeros((L, RP), f32), jnp.tile(sin, (cfg.DEC_BATCH, 1))], 0)
    return cos_t, sin_t


def _mla_weights(w_a, w_uq, w_ukv, cfg):
    D, H, QL, KVL, NP, RP, VD = cfg.D, cfg.H, cfg.QL, cfg.KVL, cfg.NOPE, cfg.ROPE, cfg.VD
    kr_w = w_a[:, QL + KVL:]
    z = jnp.zeros((D, 128 - RP), f32)
    w_a2 = jnp.concatenate([w_a[:, :QL + KVL], kr_w, z, _rot_cols(kr_w), z], axis=1).astype(bf16)
    wq = w_uq.reshape(QL, H, NP + RP)
    w_q = jnp.concatenate([wq, _rot_cols(wq[..., NP:])], axis=-1).transpose(1, 0, 2).astype(bf16)
    w_kv = w_ukv.reshape(KVL, H, NP + VD).transpose(1, 0, 2).astype(bf16)
    return w_a2, w_q, w_kv


def _forward(cfg, x_prompt, x_sample, c, cache_ckv, cache_krope, c_ctx, w_ada, b_ada, w_f_in, w_f_out,
             w_mla_a, g_mla_q, g_mla_kv, w_mla_uq, w_mla_ukv, w_mla_o, ln_mix_g, ln_mix_b,
             ln_ffn_g, ln_ffn_b, w_router, b_router, w_exp_in, w_exp_out):
    assert cfg.BATCH * cfg.SEQ == cfg.seg_len and cfg.R * cfg.R == cfg.seg_len
    assert cfg.TQ % cfg.SEQ == 0 and cfg.LK % cfg.TKV == 0
    assert cfg.R % cfg.FFT_J == 0 and cfg.R % cfg.FFT_KB == 0
    N, D, L, E = cfg.N, cfg.D, cfg.seg_len, cfg.E
    assert cfg.DEPTH >= 2
    y = (x_prompt.reshape(L, D), x_sample.reshape(cfg.DEC_BATCH * L, D))

    cvec = jnp.concatenate([c_ctx[None], c, jnp.zeros((MOD_ROWS - cfg.n_seg, D), f32)], axis=0)
    mod = _ada_call(cvec, w_ada, b_ada, cfg)
    modr = mod.reshape(cfg.DEPTH * MOD_ROWS * 6, 1, D)

    tabs = _dft_tables(cfg)
    cos_t, sin_t = _rope_tables(cfg)
    w_hi = w_router.astype(bf16)
    w_lo = (w_router - w_hi.astype(f32)).astype(bf16)
    wr2 = jnp.concatenate([jnp.concatenate([w_hi, w_lo], 1),
                           jnp.concatenate([w_hi, jnp.zeros_like(w_lo)], 1)], 0)
    rc = cfg.RC
    tri = jnp.asarray(np.triu(np.ones((rc, rc), np.float32)), bf16)

    ckv_out, kr_out = [], []
    xs_buf = jnp.zeros((2 * N + E * cfg.TE, D // 2), jnp.uint32)
    for l in range(cfg.DEPTH):
        j = l // 2
        if l % 2 == 0:
            yt = y if isinstance(y, tuple) else (y,)
            a = _fourier_in_call(yt, modr, l, w_f_in[j].astype(bf16), tabs["chan"], cfg)
            f_p = _prompt_dft_call(a, tabs["c_seq"], tabs["s_seq"], cfg)
            t = _fft_stage1_call(a, tabs["k1m"], tabs["tw_re"], tabs["tw_im"], cfg)
            f_s = _fft_stage2_call(t, tabs["k2c"], tabs["k2s"], cfg)
            y = _post_call((f_p, f_s), w_f_out[j].astype(bf16), yt, modr, l, 2, ln_mix_g[l], ln_mix_b[l], cfg)
        else:
            w_a2, w_q, w_kv = _mla_weights(w_mla_a[j], w_mla_uq[j], w_mla_ukv[j], cfg)
            cq, ckv, kr, krr = _mla_a_call(y, modr, l, w_a2, g_mla_q[j], g_mla_kv[j], cos_t, sin_t, cfg)
            ckv_out.append(ckv[:L].reshape(cfg.BATCH, cfg.SEQ, cfg.KVL))
            kr_out.append(kr[:L].reshape(cfg.BATCH, cfg.SEQ, cfg.ROPE))
            q = _q_call(cq, w_q, cos_t, sin_t, cfg)
            pad = cfg.PAST
            ckv_all = jnp.concatenate([
                jnp.concatenate([ckv[:L], jnp.zeros((pad, cfg.KVL), f32)], 0)[None],
                jnp.concatenate([cache_ckv[:, j], ckv[L:].reshape(cfg.DEC_BATCH, L, cfg.KVL)], 1)], 0)
            kr_all = jnp.concatenate([
                jnp.concatenate([krr[:L], jnp.zeros((pad, cfg.ROPE), f32)], 0)[None],
                jnp.concatenate([cache_krope[:, j], krr[L:].reshape(cfg.DEC_BATCH, L, cfg.ROPE)], 1)], 0)
            k, v = _kv_call(ckv_all, kr_all, w_kv, cfg)
            o = _attention_call(q, k, v, cfg)
            y = _post_call((o,), w_mla_o[j].astype(bf16), (y,), modr, l, 2, ln_mix_g[l], ln_mix_b[l], cfg)
        y, xs_buf = _moe_layer(y, xs_buf, modr, l, wr2, b_router, tri, w_exp_in, w_exp_out, ln_ffn_g[l],
                               ln_ffn_b[l], cfg, split_out=(l == cfg.DEPTH - 1))

    y_p = y[0].reshape(cfg.BATCH, cfg.SEQ, D)
    y_s = y[1].reshape(cfg.DEC_BATCH, L, D)
    return y_p, y_s, jnp.stack(ckv_out, axis=1), jnp.stack(kr_out, axis=1)


def kernel(x_prompt, x_sample, c, cache_ckv, cache_krope, c_ctx, w_ada, b_ada, w_f_in, w_f_out, w_mla_a, g_mla_q, g_mla_kv, w_mla_uq, w_mla_ukv, w_mla_o, ln_mix_g, ln_mix_b, ln_ffn_g, ln_ffn_b, w_router, b_router, w_exp_in, w_exp_out):
    return _forward(Cfg(), x_prompt, x_sample, c, cache_ckv, cache_krope, c_ctx, w_ada, b_ada, w_f_in,
                    w_f_out, w_mla_a, g_mla_q, g_mla_kv, w_mla_uq, w_mla_ukv, w_mla_o, ln_mix_g,
                    ln_mix_b, ln_ffn_g, ln_ffn_b, w_router, b_router, w_exp_in, w_exp_out)
```

```python
import functools
import math
from typing import NamedTuple

import numpy as np
import jax
import jax.numpy as jnp
from jax import lax
from jax.experimental import pallas as pl
from jax.experimental.pallas import tpu as pltpu

f32 = jnp.float32
bf16 = jnp.bfloat16
i32 = jnp.int32

LN_EPS = 1e-5
RMS_EPS = 1e-6
ROPE_BASE = 10000.0
ROUTED_SCALE = 2.5
LOG2E = 1.4426950408889634
NEG_BIG = -1e30
MOD_ROWS = 16
VMEM_LIMIT_V7X = 56 * 1024 * 1024


class Cfg(NamedTuple):
    D: int = 2048
    DEPTH: int = 4
    BATCH: int = 16
    SEQ: int = 256
    DEC_BATCH: int = 8
    DEC_SEQ: int = 4096
    PAST: int = 256
    GRID_W: int = 64
    F_GROUPS: int = 4
    H: int = 16
    QL: int = 512
    KVL: int = 512
    NOPE: int = 128
    ROPE: int = 64
    VD: int = 128
    E: int = 64
    NG: int = 8
    DE: int = 512
    TM: int = 512
    TN_ADA: int = 1024
    FFT_J: int = 8
    FFT_KB: int = 8
    TKV: int = 256
    TQ: int = 512
    HB: int = 2
    RC: int = 1024
    TE: int = 256

    @property
    def seg_len(self):
        return self.DEC_SEQ

    @property
    def n_seg(self):
        return 1 + self.DEC_BATCH

    @property
    def N(self):
        return self.n_seg * self.seg_len

    @property
    def R(self):
        return int(round(math.sqrt(self.seg_len)))

    @property
    def LK(self):
        return self.PAST + self.seg_len

    @property
    def alpha(self):
        return (2 * self.DEPTH) ** 0.25


def _cparams(sem):
    return pltpu.CompilerParams(dimension_semantics=sem, vmem_limit_bytes=VMEM_LIMIT_V7X)


def _ln(x):
    mu = jnp.mean(x, axis=-1, keepdims=True)
    xc = x - mu
    var = jnp.mean(xc * xc, axis=-1, keepdims=True)
    return xc * lax.rsqrt(var + LN_EPS)


def _sigmoid(x):
    return 1.0 / (1.0 + jnp.exp(-x))


def _dot(a, b):
    return jnp.dot(a, b, preferred_element_type=f32)


def _pack2(hi, lo):
    h = lax.bitcast_convert_type(hi.astype(bf16).astype(f32), jnp.uint32)
    l = lax.bitcast_convert_type(lo.astype(bf16).astype(f32), jnp.uint32)
    return h | (l >> 16)


def _unpack2_f32(w):
    return (lax.bitcast_convert_type(w & jnp.uint32(0xFFFF0000), f32),
            lax.bitcast_convert_type(w << 16, f32))


def _unpack2(w):
    hi, lo = _unpack2_f32(w)
    return hi.astype(bf16), lo.astype(bf16)


def _mod_spec(cfg, layer, which, tm):
    spt = cfg.seg_len // tm
    base = layer * MOD_ROWS * 6
    return pl.BlockSpec((1, 1, cfg.D), lambda i: (base + (i // spt) * 6 + which, 0, 0))


def _rows_specs(n_arrs, tm, cfg, width):
    spt = cfg.seg_len // tm
    if n_arrs == 2:
        return [pl.BlockSpec((tm, width), lambda i: (jnp.minimum(i, spt - 1), 0)),
                pl.BlockSpec((tm, width), lambda i: (jnp.maximum(i - spt, 0), 0))]
    return [pl.BlockSpec((tm, width), lambda i: (i, 0))]


def _rows_read(refs, spt):
    if len(refs) == 2:
        return jnp.where(pl.program_id(0) < spt, refs[0][...], refs[1][...])
    return refs[0][...]


def _ada_call(cvec, w_ada, b_ada, cfg):
    D, n6, tn = cfg.D, 6 * cfg.D, cfg.TN_ADA

    def kern(c_ref, w_ref, b_ref, o_ref):
        c = c_ref[...]
        s = (c * _sigmoid(c)).astype(bf16)
        o_ref[0] = _dot(s, w_ref[0].astype(bf16)) + b_ref[0]

    return pl.pallas_call(
        kern,
        grid=(cfg.DEPTH, n6 // tn),
        in_specs=[pl.BlockSpec((MOD_ROWS, D), lambda l, j: (0, 0)),
                  pl.BlockSpec((1, D, tn), lambda l, j: (l, 0, j)),
                  pl.BlockSpec((1, 1, tn), lambda l, j: (l, 0, j))],
        out_specs=pl.BlockSpec((1, MOD_ROWS, tn), lambda l, j: (l, 0, j)),
        out_shape=jax.ShapeDtypeStruct((cfg.DEPTH, MOD_ROWS, n6), f32),
        compiler_params=_cparams(("arbitrary", "arbitrary")),
        name="ada",
    )(cvec, w_ada, b_ada.reshape(cfg.DEPTH, 1, n6))


def _fourier_in_call(y, modr, layer, w_in, cs, cfg):
    N, D, tm = cfg.N, cfg.D, cfg.TM
    gd = D // cfg.F_GROUPS
    ny = len(y)
    spt = cfg.seg_len // tm

    def kern(*refs):
        y_refs, (sh_ref, sc_ref, w_ref, cs_ref, a_ref) = refs[:ny], refs[ny:]
        u = _ln(_rows_read(y_refs, spt)) * (1.0 + sc_ref[0]) + sh_ref[0]
        h = _dot(u.astype(bf16), w_ref[...]).astype(bf16)
        for g in range(cfg.F_GROUPS):
            a = _dot(h[:, g * gd:(g + 1) * gd], cs_ref[...])
            a_ref[:, g * gd:(g + 1) * gd] = _pack2(a[:, :gd], a[:, gd:])

    return pl.pallas_call(
        kern,
        grid=(N // tm,),
        in_specs=_rows_specs(ny, tm, cfg, D) + [
            _mod_spec(cfg, layer, 0, tm), _mod_spec(cfg, layer, 1, tm),
            pl.BlockSpec((D, D), lambda i: (0, 0)),
            pl.BlockSpec((gd, 2 * gd), lambda i: (0, 0))],
        out_specs=pl.BlockSpec((tm, D), lambda i: (i, 0)),
        out_shape=jax.ShapeDtypeStruct((N, D), jnp.uint32),
        compiler_params=_cparams(("arbitrary",)),
        name="fourier_in",
    )(*y, modr, modr, w_in, cs)


def _prompt_dft_call(a, c_seq, s_seq, cfg):
    D, L = cfg.D, cfg.SEQ
    gd = D // cfg.F_GROUPS
    a3 = a.reshape(cfg.N // L, L, D)

    def kern(c_ref, s_ref, x_ref, o_ref):
        re, im = _unpack2(x_ref[0])
        o_ref[0] = _dot(c_ref[...], re) + _dot(s_ref[...], im)

    out = pl.pallas_call(
        kern,
        grid=(cfg.BATCH, cfg.F_GROUPS),
        in_specs=[pl.BlockSpec((L, L), lambda s, g: (0, 0)),
                  pl.BlockSpec((L, L), lambda s, g: (0, 0)),
                  pl.BlockSpec((1, L, gd), lambda s, g: (s, 0, g))],
        out_specs=pl.BlockSpec((1, L, gd), lambda s, g: (s, 0, g)),
        out_shape=jax.ShapeDtypeStruct((cfg.BATCH, L, D), f32),
        compiler_params=_cparams(("arbitrary", "arbitrary")),
        name="prompt_dft",
    )(c_seq, s_seq, a3)
    return out.reshape(cfg.seg_len, D)


def _fft_stage1_call(a, k1m, tw_re, tw_im, cfg):
    D, R, J = cfg.D, cfg.R, cfg.FFT_J
    gd = D // cfg.F_GROUPS
    rj = R * J
    a4 = a.reshape(cfg.n_seg, R, R, D)

    def kern(km_ref, twr_ref, twi_ref, x_ref, o_ref):
        x = jnp.concatenate(_unpack2(x_ref[0].reshape(rj, gd)), axis=1)
        pq = _dot(km_ref[...], x)
        p, q = pq[:rj], pq[rj:]
        t_re = p[:, :gd] - q[:, gd:]
        t_im = q[:, :gd] + p[:, gd:]
        twr = jnp.tile(twr_ref[0], (1, gd // 128))
        twi = jnp.tile(twi_ref[0], (1, gd // 128))
        o_ref[0] = _pack2(t_re * twr - t_im * twi, t_re * twi + t_im * twr).reshape(R, J, gd)

    return pl.pallas_call(
        kern,
        grid=(cfg.DEC_BATCH, R // J, cfg.F_GROUPS),
        in_specs=[pl.BlockSpec((2 * rj, rj), lambda b, j, g: (0, 0)),
                  pl.BlockSpec((1, rj, 128), lambda b, j, g: (j, 0, 0)),
                  pl.BlockSpec((1, rj, 128), lambda b, j, g: (j, 0, 0)),
                  pl.BlockSpec((1, R, J, gd), lambda b, j, g: (b + 1, 0, j, g))],
        out_specs=pl.BlockSpec((1, R, J, gd), lambda b, j, g: (b, 0, j, g)),
        out_shape=jax.ShapeDtypeStruct((cfg.DEC_BATCH, R, R, D), jnp.uint32),
        compiler_params=_cparams(("arbitrary", "arbitrary", "arbitrary")),
        name="fft_stage1",
    )(k1m, tw_re, tw_im, a4)


def _fft_stage2_call(t4, k2c, k2s, cfg):
    D, R, KB = cfg.D, cfg.R, cfg.FFT_KB
    gd = D // cfg.F_GROUPS
    t3 = t4.reshape(cfg.DEC_BATCH, R * R, D)

    def kern(kc_ref, ks_ref, x_ref, o_ref):
        re, im = _unpack2(x_ref[0])
        yv = _dot(kc_ref[...], re) + _dot(ks_ref[...], im)
        o_ref[0] = yv.reshape(R, KB, gd)

    out = pl.pallas_call(
        kern,
        grid=(cfg.DEC_BATCH, R // KB, cfg.F_GROUPS),
        in_specs=[pl.BlockSpec((R * KB, KB * R), lambda b, k, g: (0, 0)),
                  pl.BlockSpec((R * KB, KB * R), lambda b, k, g: (0, 0)),
                  pl.BlockSpec((1, KB * R, gd), lambda b, k, g: (b, k, g))],
        out_specs=pl.BlockSpec((1, R, KB, gd), lambda b, k, g: (b, 0, k, g)),
        out_shape=jax.ShapeDtypeStruct((cfg.DEC_BATCH, R, R, D), f32),
        compiler_params=_cparams(("arbitrary", "arbitrary", "arbitrary")),
        name="fft_stage2",
    )(k2c, k2s, t3)
    return out.reshape(cfg.DEC_BATCH * cfg.seg_len, D)


def _post_call(fs, w_out, y, modr, layer, which, ln_g, ln_b, cfg):
    N, D, tm = cfg.N, cfg.D, cfg.TM
    spt = cfg.seg_len // tm
    nf, ny = len(fs), len(y)
    alpha = cfg.alpha

    def kern(*refs):
        f_refs, w_ref, y_refs = refs[:nf], refs[nf], refs[nf + 1:nf + 1 + ny]
        g_ref, lg_ref, lb_ref, o_ref = refs[nf + 1 + ny:]
        m = _dot(_rows_read(f_refs, spt).astype(bf16), w_ref[...])
        z = alpha * _rows_read(y_refs, spt) + g_ref[0] * m
        o_ref[...] = _ln(z) * lg_ref[...] + lb_ref[...]

    return pl.pallas_call(
        kern,
        grid=(N // tm,),
        in_specs=_rows_specs(nf, tm, cfg, D) + [pl.BlockSpec((D, D), lambda i: (0, 0))]
        + _rows_specs(ny, tm, cfg, D) + [_mod_spec(cfg, layer, which, tm),
                                         pl.BlockSpec((1, D), lambda i: (0, 0)),
                                         pl.BlockSpec((1, D), lambda i: (0, 0))],
        out_specs=pl.BlockSpec((tm, D), lambda i: (i, 0)),
        out_shape=jax.ShapeDtypeStruct((N, D), f32),
        compiler_params=_cparams(("arbitrary",)),
        name="mixer_post",
    )(*fs, w_out, *y, modr, ln_g.reshape(1, D), ln_b.reshape(1, D))


def _mla_a_call(y, modr, layer, w_a2, g_q, g_kv, cos_t, sin_t, cfg):
    N, D, tm = cfg.N, cfg.D, cfg.TM
    QL, KVL, RP = cfg.QL, cfg.KVL, cfg.ROPE
    wa = w_a2.shape[1]
    o_kr = QL + KVL
    o_rot = o_kr + 128

    def rms(x, g):
        return x * lax.rsqrt(jnp.mean(x * x, axis=-1, keepdims=True) + RMS_EPS) * g

    def kern(y_ref, sh_ref, sc_ref, w_ref, gq_ref, gkv_ref, cos_ref, sin_ref,
             cq_ref, ckv_ref, kr_ref, krr_ref):
        u = _ln(y_ref[...]) * (1.0 + sc_ref[0]) + sh_ref[0]
        a = _dot(u.astype(bf16), w_ref[...])
        cq_ref[...] = rms(a[:, :QL], gq_ref[...]).astype(bf16)
        ckv_ref[...] = rms(a[:, QL:QL + KVL], gkv_ref[...])
        kr = a[:, o_kr:o_kr + RP]
        kr_ref[...] = kr
        krr_ref[...] = kr * cos_ref[...] + a[:, o_rot:o_rot + RP] * sin_ref[...]

    return pl.pallas_call(
        kern,
        grid=(N // tm,),
        in_specs=[pl.BlockSpec((tm, D), lambda i: (i, 0)),
                  _mod_spec(cfg, layer, 0, tm), _mod_spec(cfg, layer, 1, tm),
                  pl.BlockSpec((D, wa), lambda i: (0, 0)),
                  pl.BlockSpec((1, QL), lambda i: (0, 0)),
                  pl.BlockSpec((1, KVL), lambda i: (0, 0)),
                  pl.BlockSpec((tm, RP), lambda i: (i, 0)),
                  pl.BlockSpec((tm, RP), lambda i: (i, 0))],
        out_specs=[pl.BlockSpec((tm, QL), lambda i: (i, 0)),
                   pl.BlockSpec((tm, KVL), lambda i: (i, 0)),
                   pl.BlockSpec((tm, RP), lambda i: (i, 0)),
                   pl.BlockSpec((tm, RP), lambda i: (i, 0))],
        out_shape=[jax.ShapeDtypeStruct((N, QL), bf16),
                   jax.ShapeDtypeStruct((N, KVL), f32),
                   jax.ShapeDtypeStruct((N, RP), f32),
                   jax.ShapeDtypeStruct((N, RP), f32)],
        compiler_params=_cparams(("arbitrary",)),
        name="mla_down",
    )(y, modr, modr, w_a2, g_q.reshape(1, QL), g_kv.reshape(1, KVL), cos_t, sin_t)


def _q_call(cq, w_q, cos_t, sin_t, cfg):
    N, tm, H = cfg.N, cfg.TM, cfg.H
    QL, NP, RP = cfg.QL, cfg.NOPE, cfg.ROPE
    dk = NP + RP
    wq = NP + 2 * RP
    qscale = LOG2E / math.sqrt(dk)

    def kern(cq_ref, w_ref, cos_ref, sin_ref, q_ref):
        cq_t = cq_ref[...]
        cos, sin = cos_ref[...], sin_ref[...]
        for h in range(H):
            t = _dot(cq_t, w_ref[h])
            rope = t[:, NP:NP + RP] * cos + t[:, NP + RP:] * sin
            q_ref[h] = (jnp.concatenate([t[:, :NP], rope], axis=-1) * qscale).astype(bf16)

    return pl.pallas_call(
        kern,
        grid=(N // tm,),
        in_specs=[pl.BlockSpec((tm, QL), lambda i: (i, 0)),
                  pl.BlockSpec((H, QL, wq), lambda i: (0, 0, 0)),
                  pl.BlockSpec((tm, RP), lambda i: (i, 0)),
                  pl.BlockSpec((tm, RP), lambda i: (i, 0))],
        out_specs=pl.BlockSpec((H, tm, dk), lambda i: (0, i, 0)),
        out_shape=jax.ShapeDtypeStruct((H, N, dk), bf16),
        compiler_params=_cparams(("arbitrary",)),
        name="mla_q",
    )(cq, w_q, cos_t, sin_t)


def _kv_call(ckv_all, kr_all, w_kv, cfg):
    H, KVL, NP, RP, VD = cfg.H, cfg.KVL, cfg.NOPE, cfg.ROPE, cfg.VD
    LK, tkv, ns = cfg.LK, cfg.TKV, cfg.n_seg
    dk = NP + RP

    def kern(c_ref, kr_ref, w_ref, k_ref, v_ref):
        c = c_ref[0].astype(bf16)
        kr = kr_ref[0]
        for h in range(H):
            t = _dot(c, w_ref[h])
            k_ref[h, 0] = jnp.concatenate([t[:, :NP], kr], axis=-1).astype(bf16)
            v_ref[h, 0] = jnp.concatenate([t[:, NP:], jnp.ones((tkv, 128), f32)], axis=-1).astype(bf16)

    return pl.pallas_call(
        kern,
        grid=(ns, LK // tkv),
        in_specs=[pl.BlockSpec((1, tkv, KVL), lambda s, i: (s, i, 0)),
                  pl.BlockSpec((1, tkv, RP), lambda s, i: (s, i, 0)),
                  pl.BlockSpec((H, KVL, NP + VD), lambda s, i: (0, 0, 0))],
        out_specs=[pl.BlockSpec((H, 1, tkv, dk), lambda s, i: (0, s, i, 0)),
                   pl.BlockSpec((H, 1, tkv, VD + 128), lambda s, i: (0, s, i, 0))],
        out_shape=[jax.ShapeDtypeStruct((H, ns, LK, dk), bf16),
                   jax.ShapeDtypeStruct((H, ns, LK, VD + 128), bf16)],
        compiler_params=_cparams(("arbitrary", "arbitrary")),
        name="mla_kv",
    )(ckv_all, kr_all, w_kv)


def _attention_call(q, k, v, cfg):
    H, NP, RP, VD = cfg.H, cfg.NOPE, cfg.ROPE, cfg.VD
    assert VD == 128
    dk = NP + RP
    LK, TQ, SEQ, HB = cfg.LK, cfg.TQ, cfg.SEQ, cfg.HB
    nq = cfg.seg_len // TQ
    sub = TQ // SEQ

    def scores(qb, kb):
        return lax.dot_general(qb, kb, (((1,), (1,)), ((), ())), preferred_element_type=f32)

    def finish(s, vb):
        p = jnp.exp2(s - jnp.max(s, axis=-1, keepdims=True)).astype(bf16)
        acc = _dot(p, vb)
        return (acc[:, :VD] / acc[:, VD:]).astype(bf16)

    def kern(q_ref, k_ref, v_ref, o_ref):
        seg = pl.program_id(0)
        qi = pl.program_id(2)

        @pl.when(seg == 0)
        def _():
            for hh in range(HB):
                for j in range(sub):
                    off = pl.multiple_of((qi * sub + j) * SEQ, SEQ)
                    s = scores(q_ref[hh, j * SEQ:(j + 1) * SEQ, :], k_ref[hh, 0, pl.ds(off, SEQ), :])
                    o_ref[j * SEQ:(j + 1) * SEQ, hh * VD:(hh + 1) * VD] = finish(
                        s, v_ref[hh, 0, pl.ds(off, SEQ), :])

        @pl.when(seg != 0)
        def _():
            ss = [scores(q_ref[hh], k_ref[hh, 0]) for hh in range(HB)]
            for hh in range(HB):
                o_ref[:, hh * VD:(hh + 1) * VD] = finish(ss[hh], v_ref[hh, 0])

    return pl.pallas_call(
        kern,
        grid=(cfg.n_seg, H // HB, nq),
        in_specs=[pl.BlockSpec((HB, TQ, dk), lambda s, h, i: (h, s * nq + i, 0)),
                  pl.BlockSpec((HB, 1, LK, dk), lambda s, h, i: (h, s, 0, 0)),
                  pl.BlockSpec((HB, 1, LK, VD + 128), lambda s, h, i: (h, s, 0, 0))],
        out_specs=pl.BlockSpec((TQ, HB * VD), lambda s, h, i: (s * nq + i, h)),
        out_shape=jax.ShapeDtypeStruct((cfg.N, H * VD), bf16),
        compiler_params=_cparams(("arbitrary", "arbitrary", "arbitrary")),
        name="mla_attention",
    )(q, k, v)


def _router_call(y, modr, layer, wr2, b_router, cfg):
    N, D, tm, E, NG = cfg.N, cfg.D, cfg.TM, cfg.E, cfg.NG
    epg = E // NG

    def top2(x, jio):
        m1 = jnp.max(x, axis=0, keepdims=True)
        i1 = jnp.min(jnp.where(x == m1, jio, epg), axis=0, keepdims=True)
        x2 = jnp.where(jio == i1, -jnp.inf, x)
        m2 = jnp.max(x2, axis=0, keepdims=True)
        i2 = jnp.min(jnp.where(x2 == m2, jio, epg), axis=0, keepdims=True)
        return m1, i1, m2, i2

    def kern(y_ref, sh_ref, sc_ref, w_ref, b_ref, u_ref, e_ref, g_ref, cnt_ref):
        i = pl.program_id(0)
        u = _ln(y_ref[...]) * (1.0 + sc_ref[0]) + sh_ref[0]
        u_hi = u.astype(bf16)
        u_ref[...] = _pack2(u[:, :D // 2], u[:, D // 2:])
        u_lo = (u - u_hi.astype(f32)).astype(bf16)
        out = _dot(jnp.concatenate([u_hi, u_lo], axis=1), w_ref[...])
        lt = out.T
        logits = lt[:E] + lt[E:]
        scores = _sigmoid(logits)
        biased = scores + b_ref[...]
        jio = lax.broadcasted_iota(i32, (epg, tm), 0)
        best = gi = None
        for g in range(NG):
            m1, _, m2, _ = top2(biased[g * epg:(g + 1) * epg], jio)
            gs = m1 + m2
            if g == 0:
                best, gi = gs, jnp.zeros((1, tm), i32)
            else:
                better = gs > best
                best = jnp.where(better, gs, best)
                gi = jnp.where(better, g, gi)
        sel_b = biased[:epg]
        sel_s = scores[:epg]
        for g in range(1, NG):
            pick = gi == g
            sel_b = jnp.where(pick, biased[g * epg:(g + 1) * epg], sel_b)
            sel_s = jnp.where(pick, scores[g * epg:(g + 1) * epg], sel_s)
        _, i1, _, i2 = top2(sel_b, jio)
        s1 = jnp.sum(jnp.where(jio == i1, sel_s, 0.0), axis=0, keepdims=True)
        s2 = jnp.sum(jnp.where(jio == i2, sel_s, 0.0), axis=0, keepdims=True)
        den = s1 + s2
        e1 = gi * epg + i1
        e2 = gi * epg + i2
        e_ref[0:1, :] = e1
        e_ref[1:2, :] = e2
        rio = lax.broadcasted_iota(i32, (128, tm), 0)
        gates = jnp.where(rio == 0, ROUTED_SCALE * s1 / den,
                          jnp.where(rio == 1, ROUTED_SCALE * s2 / den, 0.0))
        g_ref[...] = gates.T
        eio = lax.broadcasted_iota(i32, (E, tm), 0)
        hits = jnp.where(eio == e1, 1.0, 0.0) + jnp.where(eio == e2, 1.0, 0.0)
        part = jnp.sum(hits, axis=1, keepdims=True)

        @pl.when(i == 0)
        def _():
            cnt_ref[...] = jnp.zeros_like(cnt_ref)

        cnt_ref[...] += part

    return pl.pallas_call(
        kern,
        grid=(N // tm,),
        in_specs=[pl.BlockSpec((tm, D), lambda i: (i, 0)),
                  _mod_spec(cfg, layer, 3, tm), _mod_spec(cfg, layer, 4, tm),
                  pl.BlockSpec((2 * D, 2 * E), lambda i: (0, 0)),
                  pl.BlockSpec((E, 1), lambda i: (0, 0))],
        out_specs=[pl.BlockSpec((tm, D // 2), lambda i: (i, 0)),
                   pl.BlockSpec((2, tm), lambda i: (0, i)),
                   pl.BlockSpec((tm, 128), lambda i: (i, 0)),
                   pl.BlockSpec((E, 128), lambda i: (0, 0))],
        out_shape=[jax.ShapeDtypeStruct((N, D // 2), jnp.uint32),
                   jax.ShapeDtypeStruct((2, N), i32),
                   jax.ShapeDtypeStruct((N, 128), f32),
                   jax.ShapeDtypeStruct((E, 128), f32)],
        compiler_params=_cparams(("arbitrary",)),
        name="moe_router",
    )(y, modr, modr, wr2, b_router.reshape(E, 1))


def _rank_call(e_flat, pstart, tri, cfg):
    E, C = cfg.E, cfg.RC
    n_asg = e_flat.shape[1]

    def kern(e_ref, ps_ref, tri_ref, d_ref, base_ref):
        @pl.when(pl.program_id(0) == 0)
        def _():
            base_ref[...] = jnp.zeros_like(base_ref)

        eio = lax.broadcasted_iota(i32, (E, C), 0)
        oh = eio == e_ref[...]
        incl = _dot(jnp.where(oh, 1.0, 0.0).astype(bf16), tri_ref[...])
        base = base_ref[:, 0:1]
        pos = ps_ref[...] + base + incl - 1.0
        d_ref[...] = jnp.sum(jnp.where(oh, pos, 0.0), axis=0, keepdims=True).astype(i32)
        base_ref[...] += incl[:, C - 1:C]

    return pl.pallas_call(
        kern,
        grid=(n_asg // C,),
        in_specs=[pl.BlockSpec((1, C), lambda i: (0, i)),
                  pl.BlockSpec((E, 1), lambda i: (0, 0)),
                  pl.BlockSpec((C, C), lambda i: (0, 0))],
        out_specs=pl.BlockSpec((1, C), lambda i: (0, i)),
        out_shape=jax.ShapeDtypeStruct((1, n_asg), i32),
        scratch_shapes=[pltpu.VMEM((E, 128), f32)],
        compiler_params=_cparams(("arbitrary",)),
        name="moe_rank",
    )(e_flat, pstart, tri)


def _dispatch_call(dest3, u, xs0, cfg):
    N, D, tm = cfg.N, cfg.D, cfg.TM

    def kern(d_ref, u_ref, xs_in_ref, xs_ref, sem):
        del xs_in_ref

        def issue(g, c):
            for j in range(8):
                for k in range(2):
                    pltpu.make_async_copy(u_ref.at[g, pl.ds(j, 1), :],
                                          xs_ref.at[pl.ds(d_ref[0, g, k * 8 + j], 1), :], sem).start(priority=k)
            return c

        lax.fori_loop(0, tm // 8, issue, 0)
        for k in range(2):
            pltpu.make_async_copy(xs_ref.at[pl.ds(0, tm), :], xs_ref.at[pl.ds(0, tm), :], sem).wait()

    return pl.pallas_call(
        kern,
        grid=(N // tm,),
        in_specs=[pl.BlockSpec((1, tm // 8, 16), lambda i: (i, 0, 0), memory_space=pltpu.SMEM),
                  pl.BlockSpec((tm // 8, 8, D // 2), lambda i: (i, 0, 0)),
                  pl.BlockSpec(memory_space=pl.ANY)],
        out_specs=pl.BlockSpec(memory_space=pl.ANY),
        out_shape=jax.ShapeDtypeStruct(xs0.shape, xs0.dtype),
        scratch_shapes=[pltpu.SemaphoreType.DMA(())],
        input_output_aliases={2: 0},
        compiler_params=_cparams(("arbitrary",)),
        name="moe_dispatch",
    )(dest3, u.reshape(N // 8, 8, D // 2), xs0)


def _expert_call(tile_e, tile_src, tile_used, tile_next, tile_slot, xs, w_exp_in, w_exp_out, layer, cfg):
    D, DE, TE = cfg.D, cfg.DE, cfg.TE
    n_rows = xs.shape[0]
    n_tiles = n_rows // TE

    def kern(te_ref, ts_ref, tu_ref, tn_ref, tl_ref, x_ref, wi_hbm, wo_hbm, y_ref,
             wi_f, wo_f, wi_bf, wo_bf, sem):
        i = pl.program_id(0)
        e = te_ref[i]
        slot = tl_ref[i]
        first = (i == 0) | (e != te_ref[jnp.maximum(i - 1, 0)])

        def copies(expert, s):
            return (pltpu.make_async_copy(wi_hbm.at[layer, expert], wi_f.at[s], sem.at[0, s]),
                    pltpu.make_async_copy(wo_hbm.at[layer, expert], wo_f.at[s], sem.at[1, s]))

        @pl.when(i == 0)
        def _():
            for cp in copies(e, slot):
                cp.start()

        @pl.when(first)
        def _():
            for cp in copies(e, slot):
                cp.wait()
            nxt = tn_ref[i]

            @pl.when(nxt >= 0)
            def _():
                for cp in copies(nxt, 1 - slot):
                    cp.start()

            wi_bf[...] = wi_f[slot].astype(bf16)
            wo_bf[...] = wo_f[slot].astype(bf16)

        @pl.when(tu_ref[i] == 1)
        def _():
            h = _dot(jnp.concatenate(_unpack2(x_ref[...]), axis=1), wi_bf[...])
            act = h[:, :DE] * _sigmoid(h[:, :DE]) * h[:, DE:]
            yv = _dot(act.astype(bf16), wo_bf[...])
            y_ref[...] = _pack2(yv[:, :D // 2], yv[:, D // 2:])

        @pl.when(tu_ref[i] == 0)
        def _():
            y_ref[...] = jnp.zeros_like(y_ref)

    grid_spec = pltpu.PrefetchScalarGridSpec(
        num_scalar_prefetch=5,
        grid=(n_tiles,),
        in_specs=[pl.BlockSpec((TE, D // 2), lambda i, te, ts, tu, tn, tl: (ts[i], 0)),
                  pl.BlockSpec(memory_space=pl.ANY),
                  pl.BlockSpec(memory_space=pl.ANY)],
        out_specs=pl.BlockSpec((TE, D // 2), lambda i, te, ts, tu, tn, tl: (i, 0)),
        scratch_shapes=[pltpu.VMEM((2, D, 2 * DE), f32), pltpu.VMEM((2, DE, D), f32),
                        pltpu.VMEM((D, 2 * DE), bf16), pltpu.VMEM((DE, D), bf16),
                        pltpu.SemaphoreType.DMA((2, 2))],
    )
    return pl.pallas_call(
        kern,
        grid_spec=grid_spec,
        out_shape=jax.ShapeDtypeStruct((n_rows, D // 2), jnp.uint32),
        compiler_params=_cparams(("arbitrary",)),
        name="moe_experts",
    )(tile_e, tile_src, tile_used, tile_next, tile_slot, xs, w_exp_in, w_exp_out)


def _combine_call(dest3, y, gt, ys, modr, layer, ln_g, ln_b, cfg, split_out=False):
    N, D, tm = cfg.N, cfg.D, cfg.TM
    alpha = cfg.alpha
    spt = cfg.seg_len // tm
    n_tiles = N // tm
    n_out = 2 if split_out else 1
    W = D // 2

    def kern(d_ref, dn_ref, y_ref, gt_ref, g2_ref, lg_ref, lb_ref, ys_ref, *rest):
        o_refs, (r0, r1, sem) = rest[:n_out], rest[n_out:]
        bufs = (r0, r1)
        i = pl.program_id(0)
        slot = i % 2

        def issue_tile(dref, s):
            def issue(g, c):
                for j in range(8):
                    for k in range(2):
                        pltpu.make_async_copy(ys_ref.at[pl.ds(dref[0, g, k * 8 + j], 1), :],
                                              bufs[k].at[s, g, pl.ds(j, 1), :], sem.at[s]).start(priority=k)
                return c

            lax.fori_loop(0, tm // 8, issue, 0)

        @pl.when(i == 0)
        def _():
            issue_tile(d_ref, slot)

        @pl.when(i + 1 < n_tiles)
        def _():
            issue_tile(dn_ref, 1 - slot)

        for k in range(2):
            pltpu.make_async_copy(ys_ref.at[pl.ds(0, tm), :], ys_ref.at[pl.ds(0, tm), :], sem.at[slot]).wait()
        gt_t = gt_ref[...]
        hi0, lo0 = _unpack2_f32(r0[slot].reshape(tm, W))
        hi1, lo1 = _unpack2_f32(r1[slot].reshape(tm, W))
        g0, g1 = gt_t[:, 0:1], gt_t[:, 1:2]
        f = jnp.concatenate([g0 * hi0 + g1 * hi1, g0 * lo0 + g1 * lo1], axis=1)
        z = alpha * y_ref[...] + g2_ref[0] * f
        res = _ln(z) * lg_ref[...] + lb_ref[...]
        if split_out:
            @pl.when(i < spt)
            def _():
                o_refs[0][...] = res

            @pl.when(i >= spt)
            def _():
                o_refs[1][...] = res
        else:
            o_refs[0][...] = res

    if split_out:
        out_specs = _rows_specs(2, tm, cfg, D)
        out_shape = [jax.ShapeDtypeStruct((cfg.seg_len, D), f32),
                     jax.ShapeDtypeStruct((N - cfg.seg_len, D), f32)]
    else:
        out_specs = pl.BlockSpec((tm, D), lambda i: (i, 0))
        out_shape = jax.ShapeDtypeStruct((N, D), f32)
    return pl.pallas_call(
        kern,
        grid=(n_tiles,),
        in_specs=[pl.BlockSpec((1, tm // 8, 16), lambda i: (i, 0, 0), memory_space=pltpu.SMEM),
                  pl.BlockSpec((1, tm // 8, 16), lambda i: (jnp.minimum(i + 1, n_tiles - 1), 0, 0),
                               memory_space=pltpu.SMEM),
                  pl.BlockSpec((tm, D), lambda i: (i, 0)),
                  pl.BlockSpec((tm, 128), lambda i: (i, 0)),
                  _mod_spec(cfg, layer, 5, tm),
                  pl.BlockSpec((1, D), lambda i: (0, 0)),
                  pl.BlockSpec((1, D), lambda i: (0, 0)),
                  pl.BlockSpec(memory_space=pl.ANY)],
        out_specs=out_specs,
        out_shape=out_shape,
        scratch_shapes=[pltpu.VMEM((2, tm // 8, 8, W), jnp.uint32), pltpu.VMEM((2, tm // 8, 8, W), jnp.uint32),
                        pltpu.SemaphoreType.DMA((2,))],
        compiler_params=_cparams(("arbitrary",)),
        name="moe_combine",
    )(dest3, dest3, y, gt, modr, ln_g.reshape(1, D), ln_b.reshape(1, D), ys)


def _moe_layer(y, xs_buf, modr, layer, wr2, b_router, tri, w_exp_in, w_exp_out, ln_g, ln_b, cfg, split_out):
    N, D, E, TE, tm = cfg.N, cfg.D, cfg.E, cfg.TE, cfg.TM
    u, e2, gt, cnt = _router_call(y, modr, layer, wr2, b_router, cfg)
    counts = cnt[:, 0].astype(i32)
    padded = (counts + TE - 1) // TE * TE
    pend = jnp.cumsum(padded)
    pstart = pend - padded
    n_rows = 2 * N + E * TE
    n_tiles = n_rows // TE
    tile_start = jnp.arange(n_tiles, dtype=i32) * TE
    n_used = pend[-1] // TE
    tile_src = jnp.minimum(jnp.arange(n_tiles, dtype=i32), n_used - 1)
    tile_e = jnp.minimum(jnp.sum((pend[None, :] <= (tile_src * TE)[:, None]).astype(i32), axis=1), E - 1)
    tile_used = (tile_start < pend[-1]).astype(i32)
    eids = jnp.arange(E, dtype=i32)
    nonempty = counts > 0
    later = (eids[None, :] > eids[:, None]) & nonempty[None, :]
    next_e = jnp.min(jnp.where(later, eids[None, :], E), axis=1)
    next_e = jnp.where(next_e >= E, -1, next_e)
    run_idx = jnp.cumsum(nonempty.astype(i32)) - 1
    pick = tile_e[:, None] == eids[None, :]
    tile_next = jnp.sum(jnp.where(pick, next_e[None, :], 0), axis=1)
    tile_slot = jnp.sum(jnp.where(pick, run_idx[None, :], 0), axis=1) % 2

    dest = _rank_call(e2.reshape(1, 2 * N), pstart.astype(f32).reshape(E, 1), tri, cfg)
    dest3 = dest.reshape(2, N // tm, tm // 8, 8).transpose(1, 2, 0, 3).reshape(N // tm, tm // 8, 16)
    xs = _dispatch_call(dest3, u, xs_buf, cfg)
    ys = _expert_call(tile_e, tile_src, tile_used, tile_next, tile_slot, xs, w_exp_in, w_exp_out, layer, cfg)
    return _combine_call(dest3, y, gt, ys, modr, layer, ln_g, ln_b, cfg, split_out), xs


def _dft_tables(cfg):
    D, R, L = cfg.D, cfg.R, cfg.SEQ
    gd = D // cfg.F_GROUPS

    def cs(n, scale):
        k = np.arange(n)
        ang = 2.0 * np.pi * ((k[:, None] * k[None, :]) % n) / n
        return np.cos(ang) * scale, np.sin(ang) * scale

    cc, sc = cs(gd, 1.0 / math.sqrt(gd))
    chan = jnp.asarray(np.concatenate([cc, -sc], axis=1), bf16)
    cl, sl = cs(L, 1.0 / math.sqrt(L))
    cr, sr = cs(R, 1.0 / math.sqrt(R))
    J, KB = cfg.FFT_J, cfg.FFT_KB
    f2 = np.concatenate([cr, -sr], axis=0)
    k1m = jnp.asarray(np.kron(f2, np.eye(J)), bf16)
    eye = np.eye(KB)
    k2c = jnp.asarray(np.einsum("kn,ab->kabn", cr, eye).reshape(R * KB, KB * R), bf16)
    k2s = jnp.asarray(np.einsum("kn,ab->kabn", sr, eye).reshape(R * KB, KB * R), bf16)
    k1 = np.arange(R)
    ang = 2.0 * np.pi * (k1[:, None] * k1[None, :]) / (R * R)
    ang = ang.reshape(R, R // J, J).transpose(1, 0, 2).reshape(R // J, R * J)
    tw_re = jnp.asarray(np.repeat(np.cos(ang)[:, :, None], 128, axis=2), f32)
    tw_im = jnp.asarray(np.repeat(-np.sin(ang)[:, :, None], 128, axis=2), f32)
    return dict(chan=chan, c_seq=jnp.asarray(cl, bf16), s_seq=jnp.asarray(sl, bf16), k1m=k1m,
                k2c=k2c, k2s=k2s, tw_re=tw_re, tw_im=tw_im)


def _rot_cols(w):
    shp = w.shape
    w2 = w.reshape(shp[:-1] + (shp[-1] // 2, 2))
    return jnp.stack([-w2[..., 1], w2[..., 0]], axis=-1).reshape(shp)


def _rope_tables(cfg):
    L, RP = cfg.seg_len, cfg.ROPE
    pairs = RP // 4
    rows = L // cfg.GRID_W
    t_row = jnp.repeat(jnp.arange(rows, dtype=f32), cfg.GRID_W)
    t_col = jnp.tile(jnp.arange(cfg.GRID_W, dtype=f32), rows)
    inv = ROPE_BASE ** (-jnp.arange(pairs, dtype=f32) / pairs)
    ang = jnp.concatenate([t_row[:, None] * inv, t_col[:, None] * inv], -1)
    cos = jnp.repeat(jnp.cos(ang), 2, axis=-1)
    sin = jnp.repeat(jnp.sin(ang), 2, axis=-1)
    cos_t = jnp.concatenate([jnp.ones((L, RP), f32), jnp.tile(cos, (cfg.DEC_BATCH, 1))], 0)
    sin_t = jnp.concatenate([jnp.zeros((L, RP), f32), jnp.tile(sin, (cfg.DEC_BATCH, 1))], 0)
    return cos_t, sin_t


def _mla_weights(w_a, w_uq, w_ukv, cfg):
    D, H, QL, KVL, NP, RP, VD = cfg.D, cfg.H, cfg.QL, cfg.KVL, cfg.NOPE, cfg.ROPE, cfg.VD
    kr_w = w_a[:, QL + KVL:]
    z = jnp.zeros((D, 128 - RP), f32)
    w_a2 = jnp.concatenate([w_a[:, :QL + KVL], kr_w, z, _rot_cols(kr_w), z], axis=1).astype(bf16)
    wq = w_uq.reshape(QL, H, NP + RP)
    w_q = jnp.concatenate([wq, _rot_cols(wq[..., NP:])], axis=-1).transpose(1, 0, 2).astype(bf16)
    w_kv = w_ukv.reshape(KVL, H, NP + VD).transpose(1, 0, 2).astype(bf16)
    return w_a2, w_q, w_kv


def _forward(cfg, x_prompt, x_sample, c, cache_ckv, cache_krope, c_ctx, w_ada, b_ada, w_f_in, w_f_out,
             w_mla_a, g_mla_q, g_mla_kv, w_mla_uq, w_mla_ukv, w_mla_o, ln_mix_g, ln_mix_b,
             ln_ffn_g, ln_ffn_b, w_router, b_router, w_exp_in, w_exp_out):
    assert cfg.BATCH * cfg.SEQ == cfg.seg_len and cfg.R * cfg.R == cfg.seg_len
    assert cfg.TQ % cfg.SEQ == 0 and cfg.LK % cfg.TKV == 0
    assert cfg.R % cfg.FFT_J == 0 and cfg.R % cfg.FFT_KB == 0
    N, D, L, E = cfg.N, cfg.D, cfg.seg_len, cfg.E
    assert cfg.DEPTH >= 2
    y = (x_prompt.reshape(L, D), x_sample.reshape(cfg.DEC_BATCH * L, D))

    cvec = jnp.concatenate([c_ctx[None], c, jnp.zeros((MOD_ROWS - cfg.n_seg, D), f32)], axis=0)
    mod = _ada_call(cvec, w_ada, b_ada, cfg)
    modr = mod.reshape(cfg.DEPTH * MOD_ROWS * 6, 1, D)

    tabs = _dft_tables(cfg)
    cos_t, sin_t = _rope_tables(cfg)
    w_hi = w_router.astype(bf16)
    w_lo = (w_router - w_hi.astype(f32)).astype(bf16)
    wr2 = jnp.concatenate([jnp.concatenate([w_hi, w_lo], 1),
                           jnp.concatenate([w_hi, jnp.zeros_like(w_lo)], 1)], 0)
    rc = cfg.RC
    tri = jnp.asarray(np.triu(np.ones((rc, rc), np.float32)), bf16)

    ckv_out, kr_out = [], []
    xs_buf = jnp.zeros((2 * N + E * cfg.TE, D // 2), jnp.uint32)
    for l in range(cfg.DEPTH):
        j = l // 2
        if l % 2 == 0:
            yt = y if isinstance(y, tuple) else (y,)
            a = _fourier_in_call(yt, modr, l, w_f_in[j].astype(bf16), tabs["chan"], cfg)
            f_p = _prompt_dft_call(a, tabs["c_seq"], tabs["s_seq"], cfg)
            t = _fft_stage1_call(a, tabs["k1m"], tabs["tw_re"], tabs["tw_im"], cfg)
            f_s = _fft_stage2_call(t, tabs["k2c"], tabs["k2s"], cfg)
            y = _post_call((f_p, f_s), w_f_out[j].astype(bf16), yt, modr, l, 2, ln_mix_g[l], ln_mix_b[l], cfg)
        else:
            w_a2, w_q, w_kv = _mla_weights(w_mla_a[j], w_mla_uq[j], w_mla_ukv[j], cfg)
            cq, ckv, kr, krr = _mla_a_call(y, modr, l, w_a2, g_mla_q[j], g_mla_kv[j], cos_t, sin_t, cfg)
            ckv_out.append(ckv[:L].reshape(cfg.BATCH, cfg.SEQ, cfg.KVL))
            kr_out.append(kr[:L].reshape(cfg.BATCH, cfg.SEQ, cfg.ROPE))
            q = _q_call(cq, w_q, cos_t, sin_t, cfg)
            pad = cfg.PAST
            ckv_all = jnp.concatenate([
                jnp.concatenate([ckv[:L], jnp.zeros((pad, cfg.KVL), f32)], 0)[None],
                jnp.concatenate([cache_ckv[:, j], ckv[L:].reshape(cfg.DEC_BATCH, L, cfg.KVL)], 1)], 0)
            kr_all = jnp.concatenate([
                jnp.concatenate([krr[:L], jnp.zeros((pad, cfg.ROPE), f32)], 0)[None],
                jnp.concatenate([cache_krope[:, j], krr[L:].reshape(cfg.DEC_BATCH, L, cfg.ROPE)], 1)], 0)
            k, v = _kv_call(ckv_all, kr_all, w_kv, cfg)
            o = _attention_call(q, k, v, cfg)
            y = _post_call((o,), w_mla_o[j].astype(bf16), (y,), modr, l, 2, ln_mix_g[l], ln_mix_b[l], cfg)
        y, xs_buf = _moe_layer(y, xs_buf, modr, l, wr2, b_router, tri, w_exp_in, w_exp_out, ln_ffn_g[l],
                               ln_ffn_b[l], cfg, split_out=(l == cfg.DEPTH - 1))

    y_p = y[0].reshape(cfg.BATCH, cfg.SEQ, D)
    y_s = y[1].reshape(cfg.DEC_BATCH, L, D)
    return y_p, y_s, jnp.stack(ckv_out, axis=1), jnp.stack(kr_out, axis=1)


def kernel(x_prompt, x_sample, c, cache_ckv, cache_krope, c_ctx, w_ada, b_ada, w_f_in, w_f_out, w_mla_a, g_mla_q, g_mla_kv, w_mla_uq, w_mla_ukv, w_mla_o, ln_mix_g, ln_mix_b, ln_ffn_g, ln_ffn_b, w_router, b_router, w_exp_in, w_exp_out):
    return _forward(Cfg(), x_prompt, x_sample, c, cache_ckv, cache_krope, c_ctx, w_ada, b_ada, w_f_in,
                    w_f_out, w_mla_a, g_mla_q, g_mla_kv, w_mla_uq, w_mla_ukv, w_mla_o, ln_mix_g,
                    ln_mix_b, ln_ffn_g, ln_ffn_b, w_router, b_router, w_exp_in, w_exp_out)
```

```python
import functools
import math
from typing import NamedTuple

import numpy as np
import jax
import jax.numpy as jnp
from jax import lax
from jax.experimental import pallas as pl
from jax.experimental.pallas import tpu as pltpu

f32 = jnp.float32
bf16 = jnp.bfloat16
i32 = jnp.int32

LN_EPS = 1e-5
RMS_EPS = 1e-6
ROPE_BASE = 10000.0
ROUTED_SCALE = 2.5
LOG2E = 1.4426950408889634
NEG_BIG = -1e30
MOD_ROWS = 16
VMEM_LIMIT_V7X = 56 * 1024 * 1024


class Cfg(NamedTuple):
    D: int = 2048
    DEPTH: int = 4
    BATCH: int = 16
    SEQ: int = 256
    DEC_BATCH: int = 8
    DEC_SEQ: int = 4096
    PAST: int = 256
    GRID_W: int = 64
    F_GROUPS: int = 4
    H: int = 16
    QL: int = 512
    KVL: int = 512
    NOPE: int = 128
    ROPE: int = 64
    VD: int = 128
    E: int = 64
    NG: int = 8
    DE: int = 512
    TM: int = 512
    TN_ADA: int = 1024
    FFT_J: int = 8
    FFT_KB: int = 8
    TKV: int = 256
    TQ: int = 512
    HB: int = 2
    RC: int = 1024
    TE: int = 256

    @property
    def seg_len(self):
        return self.DEC_SEQ

    @property
    def n_seg(self):
        return 1 + self.DEC_BATCH

    @property
    def N(self):
        return self.n_seg * self.seg_len

    @property
    def R(self):
        return int(round(math.sqrt(self.seg_len)))

    @property
    def LK(self):
        return self.PAST + self.seg_len

    @property
    def alpha(self):
        return (2 * self.DEPTH) ** 0.25


def _cparams(sem):
    return pltpu.CompilerParams(dimension_semantics=sem, vmem_limit_bytes=VMEM_LIMIT_V7X)


def _ln(x):
    mu = jnp.mean(x, axis=-1, keepdims=True)
    xc = x - mu
    var = jnp.mean(xc * xc, axis=-1, keepdims=True)
    return xc * lax.rsqrt(var + LN_EPS)


def _sigmoid(x):
    return 1.0 / (1.0 + jnp.exp(-x))


def _dot(a, b):
    return jnp.dot(a, b, preferred_element_type=f32)


def _pack2(hi, lo):
    h = lax.bitcast_convert_type(hi.astype(bf16).astype(f32), jnp.uint32)
    l = lax.bitcast_convert_type(lo.astype(bf16).astype(f32), jnp.uint32)
    return h | (l >> 16)


def _unpack2_f32(w):
    return (lax.bitcast_convert_type(w & jnp.uint32(0xFFFF0000), f32),
            lax.bitcast_convert_type(w << 16, f32))


def _unpack2(w):
    hi, lo = _unpack2_f32(w)
    return hi.astype(bf16), lo.astype(bf16)


def _mod_spec(cfg, layer, which, tm):
    spt = cfg.seg_len // tm
    base = layer * MOD_ROWS * 6
    return pl.BlockSpec((1, 1, cfg.D), lambda i: (base + (i // spt) * 6 + which, 0, 0))


def _rows_specs(n_arrs, tm, cfg, width):
    spt = cfg.seg_len // tm
    if n_arrs == 2:
        return [pl.BlockSpec((tm, width), lambda i: (jnp.minimum(i, spt - 1), 0)),
                pl.BlockSpec((tm, width), lambda i: (jnp.maximum(i - spt, 0), 0))]
    return [pl.BlockSpec((tm, width), lambda i: (i, 0))]


def _rows_read(refs, spt):
    if len(refs) == 2:
        return jnp.where(pl.program_id(0) < spt, refs[0][...], refs[1][...])
    return refs[0][...]


def _ada_call(cvec, w_ada, b_ada, cfg):
    D, n6, tn = cfg.D, 6 * cfg.D, cfg.TN_ADA

    def kern(c_ref, w_ref, b_ref, o_ref):
        c = c_ref[...]
        s = (c * _sigmoid(c)).astype(bf16)
        o_ref[0] = _dot(s, w_ref[0].astype(bf16)) + b_ref[0]

    return pl.pallas_call(
        kern,
        grid=(cfg.DEPTH, n6 // tn),
        in_specs=[pl.BlockSpec((MOD_ROWS, D), lambda l, j: (0, 0)),
                  pl.BlockSpec((1, D, tn), lambda l, j: (l, 0, j)),
                  pl.BlockSpec((1, 1, tn), lambda l, j: (l, 0, j))],
        out_specs=pl.BlockSpec((1, MOD_ROWS, tn), lambda l, j: (l, 0, j)),
        out_shape=jax.ShapeDtypeStruct((cfg.DEPTH, MOD_ROWS, n6), f32),
        compiler_params=_cparams(("arbitrary", "arbitrary")),
        name="ada",
    )(cvec, w_ada, b_ada.reshape(cfg.DEPTH, 1, n6))


def _fourier_in_call(y, modr, layer, w_in, cs, cfg):
    N, D, tm = cfg.N, cfg.D, cfg.TM
    gd = D // cfg.F_GROUPS
    ny = len(y)
    spt = cfg.seg_len // tm

    def kern(*refs):
        y_refs, (sh_ref, sc_ref, w_ref, cs_ref, a_ref) = refs[:ny], refs[ny:]
        u = _ln(_rows_read(y_refs, spt)) * (1.0 + sc_ref[0]) + sh_ref[0]
        h = _dot(u.astype(bf16), w_ref[...]).astype(bf16)
        for g in range(cfg.F_GROUPS):
            a = _dot(h[:, g * gd:(g + 1) * gd], cs_ref[...])
            a_ref[:, g * gd:(g + 1) * gd] = _pack2(a[:, :gd], a[:, gd:])

    return pl.pallas_call(
        kern,
        grid=(N // tm,),
        in_specs=_rows_specs(ny, tm, cfg, D) + [
            _mod_spec(cfg, layer, 0, tm), _mod_spec(cfg, layer, 1, tm),
            pl.BlockSpec((D, D), lambda i: (0, 0)),
            pl.BlockSpec((gd, 2 * gd), lambda i: (0, 0))],
        out_specs=pl.BlockSpec((tm, D), lambda i: (i, 0)),
        out_shape=jax.ShapeDtypeStruct((N, D), jnp.uint32),
        compiler_params=_cparams(("arbitrary",)),
        name="fourier_in",
    )(*y, modr, modr, w_in, cs)


def _prompt_dft_call(a, c_seq, s_seq, cfg):
    D, L = cfg.D, cfg.SEQ
    gd = D // cfg.F_GROUPS
    a3 = a.reshape(cfg.N // L, L, D)

    def kern(c_ref, s_ref, x_ref, o_ref):
        re, im = _unpack2(x_ref[0])
        o_ref[0] = _dot(c_ref[...], re) + _dot(s_ref[...], im)

    out = pl.pallas_call(
        kern,
        grid=(cfg.BATCH, cfg.F_GROUPS),
        in_specs=[pl.BlockSpec((L, L), lambda s, g: (0, 0)),
                  pl.BlockSpec((L, L), lambda s, g: (0, 0)),
                  pl.BlockSpec((1, L, gd), lambda s, g: (s, 0, g))],
        out_specs=pl.BlockSpec((1, L, gd), lambda s, g: (s, 0, g)),
        out_shape=jax.ShapeDtypeStruct((cfg.BATCH, L, D), f32),
        compiler_params=_cparams(("arbitrary", "arbitrary")),
        name="prompt_dft",
    )(c_seq, s_seq, a3)
    return out.reshape(cfg.seg_len, D)


def _fft_stage1_call(a, k1m, tw_re, tw_im, cfg):
    D, R, J = cfg.D, cfg.R, cfg.FFT_J
    gd = D // cfg.F_GROUPS
    rj = R * J
    a4 = a.reshape(cfg.n_seg, R, R, D)

    def kern(km_ref, twr_ref, twi_ref, x_ref, o_ref):
        x = jnp.concatenate(_unpack2(x_ref[0].reshape(rj, gd)), axis=1)
        pq = _dot(km_ref[...], x)
        p, q = pq[:rj], pq[rj:]
        t_re = p[:, :gd] - q[:, gd:]
        t_im = q[:, :gd] + p[:, gd:]
        twr = jnp.tile(twr_ref[0], (1, gd // 128))
        twi = jnp.tile(twi_ref[0], (1, gd // 128))
        o_ref[0] = _pack2(t_re * twr - t_im * twi, t_re * twi + t_im * twr).reshape(R, J, gd)

    return pl.pallas_call(
        kern,
        grid=(cfg.DEC_BATCH, R // J, cfg.F_GROUPS),
        in_specs=[pl.BlockSpec((2 * rj, rj), lambda b, j, g: (0, 0)),
                  pl.BlockSpec((1, rj, 128), lambda b, j, g: (j, 0, 0)),
                  pl.BlockSpec((1, rj, 128), lambda b, j, g: (j, 0, 0)),
                  pl.BlockSpec((1, R, J, gd), lambda b, j, g: (b + 1, 0, j, g))],
        out_specs=pl.BlockSpec((1, R, J, gd), lambda b, j, g: (b, 0, j, g)),
        out_shape=jax.ShapeDtypeStruct((cfg.DEC_BATCH, R, R, D), jnp.uint32),
        compiler_params=_cparams(("arbitrary", "arbitrary", "arbitrary")),
        name="fft_stage1",
    )(k1m, tw_re, tw_im, a4)


def _fft_stage2_call(t4, k2c, k2s, cfg):
    D, R, KB = cfg.D, cfg.R, cfg.FFT_KB
    gd = D // cfg.F_GROUPS
    t3 = t4.reshape(cfg.DEC_BATCH, R * R, D)

    def kern(kc_ref, ks_ref, x_ref, o_ref):
        re, im = _unpack2(x_ref[0])
        yv = _dot(kc_ref[...], re) + _dot(ks_ref[...], im)
        o_ref[0] = yv.reshape(R, KB, D)

    out = pl.pallas_call(
        kern,
        grid=(cfg.DEC_BATCH, R // KB),
        in_specs=[pl.BlockSpec((R * KB, KB * R), lambda b, k: (0, 0)),
                  pl.BlockSpec((R * KB, KB * R), lambda b, k: (0, 0)),
                  pl.BlockSpec((1, KB * R, D), lambda b, k: (b, k, 0))],
        out_specs=pl.BlockSpec((1, R, KB, D), lambda b, k: (b, 0, k, 0)),
        out_shape=jax.ShapeDtypeStruct((cfg.DEC_BATCH, R, R, D), f32),
        compiler_params=_cparams(("arbitrary", "arbitrary")),
        name="fft_stage2",
    )(k2c, k2s, t3)
    return out.reshape(cfg.DEC_BATCH * cfg.seg_len, D)


def _post_call(fs, w_out, y, modr, layer, which, ln_g, ln_b, cfg):
    N, D, tm = cfg.N, cfg.D, cfg.TM
    spt = cfg.seg_len // tm
    nf, ny = len(fs), len(y)
    alpha = cfg.alpha

    def kern(*refs):
        f_refs, w_ref, y_refs = refs[:nf], refs[nf], refs[nf + 1:nf + 1 + ny]
        g_ref, lg_ref, lb_ref, o_ref = refs[nf + 1 + ny:]
        m = _dot(_rows_read(f_refs, spt).astype(bf16), w_ref[...])
        z = alpha * _rows_read(y_refs, spt) + g_ref[0] * m
        o_ref[...] = _ln(z) * lg_ref[...] + lb_ref[...]

    return pl.pallas_call(
        kern,
        grid=(N // tm,),
        in_specs=_rows_specs(nf, tm, cfg, D) + [pl.BlockSpec((D, D), lambda i: (0, 0))]
        + _rows_specs(ny, tm, cfg, D) + [_mod_spec(cfg, layer, which, tm),
                                         pl.BlockSpec((1, D), lambda i: (0, 0)),
                                         pl.BlockSpec((1, D), lambda i: (0, 0))],
        out_specs=pl.BlockSpec((tm, D), lambda i: (i, 0)),
        out_shape=jax.ShapeDtypeStruct((N, D), f32),
        compiler_params=_cparams(("arbitrary",)),
        name="mixer_post",
    )(*fs, w_out, *y, modr, ln_g.reshape(1, D), ln_b.reshape(1, D))


def _mla_a_call(y, modr, layer, w_a2, g_q, g_kv, cos_t, sin_t, cfg):
    N, D, tm = cfg.N, cfg.D, cfg.TM
    QL, KVL, RP = cfg.QL, cfg.KVL, cfg.ROPE
    wa = w_a2.shape[1]
    o_kr = QL + KVL
    o_rot = o_kr + 128

    def rms(x, g):
        return x * lax.rsqrt(jnp.mean(x * x, axis=-1, keepdims=True) + RMS_EPS) * g

    def kern(y_ref, sh_ref, sc_ref, w_ref, gq_ref, gkv_ref, cos_ref, sin_ref,
             cq_ref, ckv_ref, kr_ref, krr_ref):
        u = _ln(y_ref[...]) * (1.0 + sc_ref[0]) + sh_ref[0]
        a = _dot(u.astype(bf16), w_ref[...])
        cq_ref[...] = rms(a[:, :QL], gq_ref[...]).astype(bf16)
        ckv_ref[...] = rms(a[:, QL:QL + KVL], gkv_ref[...])
        kr = a[:, o_kr:o_kr + RP]
        kr_ref[...] = kr
        krr_ref[...] = kr * cos_ref[...] + a[:, o_rot:o_rot + RP] * sin_ref[...]

    return pl.pallas_call(
        kern,
        grid=(N // tm,),
        in_specs=[pl.BlockSpec((tm, D), lambda i: (i, 0)),
                  _mod_spec(cfg, layer, 0, tm), _mod_spec(cfg, layer, 1, tm),
                  pl.BlockSpec((D, wa), lambda i: (0, 0)),
                  pl.BlockSpec((1, QL), lambda i: (0, 0)),
                  pl.BlockSpec((1, KVL), lambda i: (0, 0)),
                  pl.BlockSpec((tm, RP), lambda i: (i, 0)),
                  pl.BlockSpec((tm, RP), lambda i: (i, 0))],
        out_specs=[pl.BlockSpec((tm, QL), lambda i: (i, 0)),
                   pl.BlockSpec((tm, KVL), lambda i: (i, 0)),
                   pl.BlockSpec((tm, RP), lambda i: (i, 0)),
                   pl.BlockSpec((tm, RP), lambda i: (i, 0))],
        out_shape=[jax.ShapeDtypeStruct((N, QL), bf16),
                   jax.ShapeDtypeStruct((N, KVL), f32),
                   jax.ShapeDtypeStruct((N, RP), f32),
                   jax.ShapeDtypeStruct((N, RP), f32)],
        compiler_params=_cparams(("arbitrary",)),
        name="mla_down",
    )(y, modr, modr, w_a2, g_q.reshape(1, QL), g_kv.reshape(1, KVL), cos_t, sin_t)


def _q_call(cq, w_q, cos_t, sin_t, cfg):
    N, tm, H = cfg.N, cfg.TM, cfg.H
    QL, NP, RP = cfg.QL, cfg.NOPE, cfg.ROPE
    dk = NP + RP
    wq = NP + 2 * RP
    qscale = LOG2E / math.sqrt(dk)

    def kern(cq_ref, w_ref, cos_ref, sin_ref, q_ref):
        cq_t = cq_ref[...]
        cos, sin = cos_ref[...], sin_ref[...]
        for h in range(H):
            t = _dot(cq_t, w_ref[h])
            rope = t[:, NP:NP + RP] * cos + t[:, NP + RP:] * sin
            q_ref[h] = (jnp.concatenate([t[:, :NP], rope], axis=-1) * qscale).astype(bf16)

    return pl.pallas_call(
        kern,
        grid=(N // tm,),
        in_specs=[pl.BlockSpec((tm, QL), lambda i: (i, 0)),
                  pl.BlockSpec((H, QL, wq), lambda i: (0, 0, 0)),
                  pl.BlockSpec((tm, RP), lambda i: (i, 0)),
                  pl.BlockSpec((tm, RP), lambda i: (i, 0))],
        out_specs=pl.BlockSpec((H, tm, dk), lambda i: (0, i, 0)),
        out_shape=jax.ShapeDtypeStruct((H, N, dk), bf16),
        compiler_params=_cparams(("arbitrary",)),
        name="mla_q",
    )(cq, w_q, cos_t, sin_t)


def _kv_call(ckv_all, kr_all, w_kv, cfg):
    H, KVL, NP, RP, VD = cfg.H, cfg.KVL, cfg.NOPE, cfg.ROPE, cfg.VD
    LK, tkv, ns = cfg.LK, cfg.TKV, cfg.n_seg
    dk = NP + RP

    def kern(c_ref, kr_ref, w_ref, k_ref, v_ref):
        c = c_ref[0].astype(bf16)
        kr = kr_ref[0]
        for h in range(H):
            t = _dot(c, w_ref[h])
            k_ref[h, 0] = jnp.concatenate([t[:, :NP], kr], axis=-1).astype(bf16)
            v_ref[h, 0] = jnp.concatenate([t[:, NP:], jnp.ones((tkv, 128), f32)], axis=-1).astype(bf16)

    return pl.pallas_call(
        kern,
        grid=(ns, LK // tkv),
        in_specs=[pl.BlockSpec((1, tkv, KVL), lambda s, i: (s, i, 0)),
                  pl.BlockSpec((1, tkv, RP), lambda s, i: (s, i, 0)),
                  pl.BlockSpec((H, KVL, NP + VD), lambda s, i: (0, 0, 0))],
        out_specs=[pl.BlockSpec((H, 1, tkv, dk), lambda s, i: (0, s, i, 0)),
                   pl.BlockSpec((H, 1, tkv, VD + 128), lambda s, i: (0, s, i, 0))],
        out_shape=[jax.ShapeDtypeStruct((H, ns, LK, dk), bf16),
                   jax.ShapeDtypeStruct((H, ns, LK, VD + 128), bf16)],
        compiler_params=_cparams(("arbitrary", "arbitrary")),
        name="mla_kv",
    )(ckv_all, kr_all, w_kv)


def _attention_call(q, k, v, cfg):
    H, NP, RP, VD = cfg.H, cfg.NOPE, cfg.ROPE, cfg.VD
    assert VD == 128
    dk = NP + RP
    LK, TQ, SEQ, HB = cfg.LK, cfg.TQ, cfg.SEQ, cfg.HB
    nq = cfg.seg_len // TQ
    sub = TQ // SEQ

    def scores(qb, kb):
        return lax.dot_general(qb, kb, (((1,), (1,)), ((), ())), preferred_element_type=f32)

    def finish(s, vb):
        p = jnp.exp2(s - jnp.max(s, axis=-1, keepdims=True)).astype(bf16)
        acc = _dot(p, vb)
        return (acc[:, :VD] / acc[:, VD:]).astype(bf16)

    def kern(q_ref, k_ref, v_ref, o_ref):
        seg = pl.program_id(0)
        qi = pl.program_id(2)

        @pl.when(seg == 0)
        def _():
            for hh in range(HB):
                for j in range(sub):
                    off = pl.multiple_of((qi * sub + j) * SEQ, SEQ)
                    s = scores(q_ref[hh, j * SEQ:(j + 1) * SEQ, :], k_ref[hh, 0, pl.ds(off, SEQ), :])
                    o_ref[j * SEQ:(j + 1) * SEQ, hh * VD:(hh + 1) * VD] = finish(
                        s, v_ref[hh, 0, pl.ds(off, SEQ), :])

        @pl.when(seg != 0)
        def _():
            ss = [scores(q_ref[hh], k_ref[hh, 0]) for hh in range(HB)]
            for hh in range(HB):
                o_ref[:, hh * VD:(hh + 1) * VD] = finish(ss[hh], v_ref[hh, 0])

    return pl.pallas_call(
        kern,
        grid=(cfg.n_seg, H // HB, nq),
        in_specs=[pl.BlockSpec((HB, TQ, dk), lambda s, h, i: (h, s * nq + i, 0)),
                  pl.BlockSpec((HB, 1, LK, dk), lambda s, h, i: (h, s, 0, 0)),
                  pl.BlockSpec((HB, 1, LK, VD + 128), lambda s, h, i: (h, s, 0, 0))],
        out_specs=pl.BlockSpec((TQ, HB * VD), lambda s, h, i: (s * nq + i, h)),
        out_shape=jax.ShapeDtypeStruct((cfg.N, H * VD), bf16),
        compiler_params=_cparams(("arbitrary", "arbitrary", "arbitrary")),
        name="mla_attention",
    )(q, k, v)


def _router_call(y, modr, layer, wr2, b_router, cfg):
    N, D, tm, E, NG = cfg.N, cfg.D, cfg.TM, cfg.E, cfg.NG
    epg = E // NG

    def top2(x, jio):
        m1 = jnp.max(x, axis=0, keepdims=True)
        i1 = jnp.min(jnp.where(x == m1, jio, epg), axis=0, keepdims=True)
        x2 = jnp.where(jio == i1, -jnp.inf, x)
        m2 = jnp.max(x2, axis=0, keepdims=True)
        i2 = jnp.min(jnp.where(x2 == m2, jio, epg), axis=0, keepdims=True)
        return m1, i1, m2, i2

    def kern(y_ref, sh_ref, sc_ref, w_ref, b_ref, u_ref, e_ref, g_ref, cnt_ref):
        i = pl.program_id(0)
        u = _ln(y_ref[...]) * (1.0 + sc_ref[0]) + sh_ref[0]
        u_hi = u.astype(bf16)
        u_ref[...] = _pack2(u[:, :D // 2], u[:, D // 2:])
        u_lo = (u - u_hi.astype(f32)).astype(bf16)
        out = _dot(jnp.concatenate([u_hi, u_lo], axis=1), w_ref[...])
        lt = out.T
        logits = lt[:E] + lt[E:]
        scores = _sigmoid(logits)
        biased = scores + b_ref[...]
        jio = lax.broadcasted_iota(i32, (epg, tm), 0)
        best = gi = None
        for g in range(NG):
            m1, _, m2, _ = top2(biased[g * epg:(g + 1) * epg], jio)
            gs = m1 + m2
            if g == 0:
                best, gi = gs, jnp.zeros((1, tm), i32)
            else:
                better = gs > best
                best = jnp.where(better, gs, best)
                gi = jnp.where(better, g, gi)
        sel_b = biased[:epg]
        sel_s = scores[:epg]
        for g in range(1, NG):
            pick = gi == g
            sel_b = jnp.where(pick, biased[g * epg:(g + 1) * epg], sel_b)
            sel_s = jnp.where(pick, scores[g * epg:(g + 1) * epg], sel_s)
        _, i1, _, i2 = top2(sel_b, jio)
        s1 = jnp.sum(jnp.where(jio == i1, sel_s, 0.0), axis=0, keepdims=True)
        s2 = jnp.sum(jnp.where(jio == i2, sel_s, 0.0), axis=0, keepdims=True)
        den = s1 + s2
        e1 = gi * epg + i1
        e2 = gi * epg + i2
        e_ref[0:1, :] = e1
        e_ref[1:2, :] = e2
        rio = lax.broadcasted_iota(i32, (128, tm), 0)
        gates = jnp.where(rio == 0, ROUTED_SCALE * s1 / den,
                          jnp.where(rio == 1, ROUTED_SCALE * s2 / den, 0.0))
        g_ref[...] = gates.T
        eio = lax.broadcasted_iota(i32, (E, tm), 0)
        hits = jnp.where(eio == e1, 1.0, 0.0) + jnp.where(eio == e2, 1.0, 0.0)
        part = jnp.sum(hits, axis=1, keepdims=True)

        @pl.when(i == 0)
        def _():
            cnt_ref[...] = jnp.zeros_like(cnt_ref)

        cnt_ref[...] += part

    return pl.pallas_call(
        kern,
        grid=(N // tm,),
        in_specs=[pl.BlockSpec((tm, D), lambda i: (i, 0)),
                  _mod_spec(cfg, layer, 3, tm), _mod_spec(cfg, layer, 4, tm),
                  pl.BlockSpec((2 * D, 2 * E), lambda i: (0, 0)),
                  pl.BlockSpec((E, 1), lambda i: (0, 0))],
        out_specs=[pl.BlockSpec((tm, D // 2), lambda i: (i, 0)),
                   pl.BlockSpec((2, tm), lambda i: (0, i)),
                   pl.BlockSpec((tm, 128), lambda i: (i, 0)),
                   pl.BlockSpec((E, 128), lambda i: (0, 0))],
        out_shape=[jax.ShapeDtypeStruct((N, D // 2), jnp.uint32),
                   jax.ShapeDtypeStruct((2, N), i32),
                   jax.ShapeDtypeStruct((N, 128), f32),
                   jax.ShapeDtypeStruct((E, 128), f32)],
        compiler_params=_cparams(("arbitrary",)),
        name="moe_router",
    )(y, modr, modr, wr2, b_router.reshape(E, 1))


def _rank_call(e_flat, pstart, tri, cfg):
    E, C = cfg.E, cfg.RC
    n_asg = e_flat.shape[1]

    def kern(e_ref, ps_ref, tri_ref, d_ref, base_ref):
        @pl.when(pl.program_id(0) == 0)
        def _():
            base_ref[...] = jnp.zeros_like(base_ref)

        eio = lax.broadcasted_iota(i32, (E, C), 0)
        oh = eio == e_ref[...]
        incl = _dot(jnp.where(oh, 1.0, 0.0).astype(bf16), tri_ref[...])
        base = base_ref[:, 0:1]
        pos = ps_ref[...] + base + incl - 1.0
        d_ref[...] = jnp.sum(jnp.where(oh, pos, 0.0), axis=0, keepdims=True).astype(i32)
        base_ref[...] += incl[:, C - 1:C]

    return pl.pallas_call(
        kern,
        grid=(n_asg // C,),
        in_specs=[pl.BlockSpec((1, C), lambda i: (0, i)),
                  pl.BlockSpec((E, 1), lambda i: (0, 0)),
                  pl.BlockSpec((C, C), lambda i: (0, 0))],
        out_specs=pl.BlockSpec((1, C), lambda i: (0, i)),
        out_shape=jax.ShapeDtypeStruct((1, n_asg), i32),
        scratch_shapes=[pltpu.VMEM((E, 128), f32)],
        compiler_params=_cparams(("arbitrary",)),
        name="moe_rank",
    )(e_flat, pstart, tri)


def _dispatch_call(dest3, u, xs0, cfg):
    N, D, tm = cfg.N, cfg.D, cfg.TM

    def kern(d_ref, u_ref, xs_in_ref, xs_ref, sem):
        del xs_in_ref

        def issue(g, c):
            for j in range(8):
                for k in range(2):
                    pltpu.make_async_copy(u_ref.at[g, pl.ds(j, 1), :],
                                          xs_ref.at[pl.ds(d_ref[0, g, k * 8 + j], 1), :], sem).start(priority=k)
            return c

        lax.fori_loop(0, tm // 8, issue, 0)
        for k in range(2):
            pltpu.make_async_copy(xs_ref.at[pl.ds(0, tm), :], xs_ref.at[pl.ds(0, tm), :], sem).wait()

    return pl.pallas_call(
        kern,
        grid=(N // tm,),
        in_specs=[pl.BlockSpec((1, tm // 8, 16), lambda i: (i, 0, 0), memory_space=pltpu.SMEM),
                  pl.BlockSpec((tm // 8, 8, D // 2), lambda i: (i, 0, 0)),
                  pl.BlockSpec(memory_space=pl.ANY)],
        out_specs=pl.BlockSpec(memory_space=pl.ANY),
        out_shape=jax.ShapeDtypeStruct(xs0.shape, xs0.dtype),
        scratch_shapes=[pltpu.SemaphoreType.DMA(())],
        input_output_aliases={2: 0},
        compiler_params=_cparams(("arbitrary",)),
        name="moe_dispatch",
    )(dest3, u.reshape(N // 8, 8, D // 2), xs0)


def _expert_call(tile_e, tile_src, tile_used, tile_next, tile_slot, xs, w_exp_in, w_exp_out, layer, cfg):
    D, DE, TE = cfg.D, cfg.DE, cfg.TE
    n_rows = xs.shape[0]
    n_tiles = n_rows // TE

    def kern(te_ref, ts_ref, tu_ref, tn_ref, tl_ref, x_ref, wi_hbm, wo_hbm, y_ref,
             wi_f, wo_f, wi_bf, wo_bf, sem):
        i = pl.program_id(0)
        e = te_ref[i]
        slot = tl_ref[i]
        first = (i == 0) | (e != te_ref[jnp.maximum(i - 1, 0)])

        def copies(expert, s):
            return (pltpu.make_async_copy(wi_hbm.at[layer, expert], wi_f.at[s], sem.at[0, s]),
                    pltpu.make_async_copy(wo_hbm.at[layer, expert], wo_f.at[s], sem.at[1, s]))

        @pl.when(i == 0)
        def _():
            for cp in copies(e, slot):
                cp.start()

        @pl.when(first)
        def _():
            for cp in copies(e, slot):
                cp.wait()
            nxt = tn_ref[i]

            @pl.when(nxt >= 0)
            def _():
                for cp in copies(nxt, 1 - slot):
                    cp.start()

            wi_bf[...] = wi_f[slot].astype(bf16)
            wo_bf[...] = wo_f[slot].astype(bf16)

        @pl.when(tu_ref[i] == 1)
        def _():
            h = _dot(jnp.concatenate(_unpack2(x_ref[...]), axis=1), wi_bf[...])
            act = h[:, :DE] * _sigmoid(h[:, :DE]) * h[:, DE:]
            yv = _dot(act.astype(bf16), wo_bf[...])
            y_ref[...] = _pack2(yv[:, :D // 2], yv[:, D // 2:])

        @pl.when(tu_ref[i] == 0)
        def _():
            y_ref[...] = jnp.zeros_like(y_ref)

    grid_spec = pltpu.PrefetchScalarGridSpec(
        num_scalar_prefetch=5,
        grid=(n_tiles,),
        in_specs=[pl.BlockSpec((TE, D // 2), lambda i, te, ts, tu, tn, tl: (ts[i], 0)),
                  pl.BlockSpec(memory_space=pl.ANY),
                  pl.BlockSpec(memory_space=pl.ANY)],
        out_specs=pl.BlockSpec((TE, D // 2), lambda i, te, ts, tu, tn, tl: (i, 0)),
        scratch_shapes=[pltpu.VMEM((2, D, 2 * DE), f32), pltpu.VMEM((2, DE, D), f32),
                        pltpu.VMEM((D, 2 * DE), bf16), pltpu.VMEM((DE, D), bf16),
                        pltpu.SemaphoreType.DMA((2, 2))],
    )
    return pl.pallas_call(
        kern,
        grid_spec=grid_spec,
        out_shape=jax.ShapeDtypeStruct((n_rows, D // 2), jnp.uint32),
        compiler_params=_cparams(("arbitrary",)),
        name="moe_experts",
    )(tile_e, tile_src, tile_used, tile_next, tile_slot, xs, w_exp_in, w_exp_out)


def _combine_call(dest3, y, gt, ys, modr, layer, ln_g, ln_b, cfg, split_out=False):
    N, D, tm = cfg.N, cfg.D, cfg.TM
    alpha = cfg.alpha
    spt = cfg.seg_len // tm
    n_tiles = N // tm
    n_out = 2 if split_out else 1
    W = D // 2

    def kern(d_ref, dn_ref, y_ref, gt_ref, g2_ref, lg_ref, lb_ref, ys_ref, *rest):
        o_refs, (r0, r1, sem) = rest[:n_out], rest[n_out:]
        bufs = (r0, r1)
        i = pl.program_id(0)
        slot = i % 2

        def issue_tile(dref, s):
            def issue(g, c):
                for j in range(8):
                    for k in range(2):
                        pltpu.make_async_copy(ys_ref.at[pl.ds(dref[0, g, k * 8 + j], 1), :],
                                              bufs[k].at[s, g, pl.ds(j, 1), :], sem.at[s]).start(priority=k)
                return c

            lax.fori_loop(0, tm // 8, issue, 0)

        @pl.when(i == 0)
        def _():
            issue_tile(d_ref, slot)

        @pl.when(i + 1 < n_tiles)
        def _():
            issue_tile(dn_ref, 1 - slot)

        for k in range(2):
            pltpu.make_async_copy(ys_ref.at[pl.ds(0, tm), :], ys_ref.at[pl.ds(0, tm), :], sem.at[slot]).wait()
        gt_t = gt_ref[...]
        hi0, lo0 = _unpack2_f32(r0[slot].reshape(tm, W))
        hi1, lo1 = _unpack2_f32(r1[slot].reshape(tm, W))
        g0, g1 = gt_t[:, 0:1], gt_t[:, 1:2]
        f = jnp.concatenate([g0 * hi0 + g1 * hi1, g0 * lo0 + g1 * lo1], axis=1)
        z = alpha * y_ref[...] + g2_ref[0] * f
        res = _ln(z) * lg_ref[...] + lb_ref[...]
        if split_out:
            @pl.when(i < spt)
            def _():
                o_refs[0][...] = res

            @pl.when(i >= spt)
            def _():
                o_refs[1][...] = res
        else:
            o_refs[0][...] = res

    if split_out:
        out_specs = _rows_specs(2, tm, cfg, D)
        out_shape = [jax.ShapeDtypeStruct((cfg.seg_len, D), f32),
                     jax.ShapeDtypeStruct((N - cfg.seg_len, D), f32)]
    else:
        out_specs = pl.BlockSpec((tm, D), lambda i: (i, 0))
        out_shape = jax.ShapeDtypeStruct((N, D), f32)
    return pl.pallas_call(
        kern,
        grid=(n_tiles,),
        in_specs=[pl.BlockSpec((1, tm // 8, 16), lambda i: (i, 0, 0), memory_space=pltpu.SMEM),
                  pl.BlockSpec((1, tm // 8, 16), lambda i: (jnp.minimum(i + 1, n_tiles - 1), 0, 0),
                               memory_space=pltpu.SMEM),
                  pl.BlockSpec((tm, D), lambda i: (i, 0)),
                  pl.BlockSpec((tm, 128), lambda i: (i, 0)),
                  _mod_spec(cfg, layer, 5, tm),
                  pl.BlockSpec((1, D), lambda i: (0, 0)),
                  pl.BlockSpec((1, D), lambda i: (0, 0)),
                  pl.BlockSpec(memory_space=pl.ANY)],
        out_specs=out_specs,
        out_shape=out_shape,
        scratch_shapes=[pltpu.VMEM((2, tm // 8, 8, W), jnp.uint32), pltpu.VMEM((2, tm // 8, 8, W), jnp.uint32),
                        pltpu.SemaphoreType.DMA((2,))],
        compiler_params=_cparams(("arbitrary",)),
        name="moe_combine",
    )(dest3, dest3, y, gt, modr, ln_g.reshape(1, D), ln_b.reshape(1, D), ys)


def _moe_layer(y, xs_buf, modr, layer, wr2, b_router, tri, w_exp_in, w_exp_out, ln_g, ln_b, cfg, split_out):
    N, D, E, TE, tm = cfg.N, cfg.D, cfg.E, cfg.TE, cfg.TM
    u, e2, gt, cnt = _router_call(y, modr, layer, wr2, b_router, cfg)
    counts = cnt[:, 0].astype(i32)
    padded = (counts + TE - 1) // TE * TE
    pend = jnp.cumsum(padded)
    pstart = pend - padded
    n_rows = 2 * N + E * TE
    n_tiles = n_rows // TE
    tile_start = jnp.arange(n_tiles, dtype=i32) * TE
    n_used = pend[-1] // TE
    tile_src = jnp.minimum(jnp.arange(n_tiles, dtype=i32), n_used - 1)
    tile_e = jnp.minimum(jnp.sum((pend[None, :] <= (tile_src * TE)[:, None]).astype(i32), axis=1), E - 1)
    tile_used = (tile_start < pend[-1]).astype(i32)
    eids = jnp.arange(E, dtype=i32)
    nonempty = counts > 0
    later = (eids[None, :] > eids[:, None]) & nonempty[None, :]
    next_e = jnp.min(jnp.where(later, eids[None, :], E), axis=1)
    next_e = jnp.where(next_e >= E, -1, next_e)
    run_idx = jnp.cumsum(nonempty.astype(i32)) - 1
    pick = tile_e[:, None] == eids[None, :]
    tile_next = jnp.sum(jnp.where(pick, next_e[None, :], 0), axis=1)
    tile_slot = jnp.sum(jnp.where(pick, run_idx[None, :], 0), axis=1) % 2

    dest = _rank_call(e2.reshape(1, 2 * N), pstart.astype(f32).reshape(E, 1), tri, cfg)
    dest3 = dest.reshape(2, N // tm, tm // 8, 8).transpose(1, 2, 0, 3).reshape(N // tm, tm // 8, 16)
    xs = _dispatch_call(dest3, u, xs_buf, cfg)
    ys = _expert_call(tile_e, tile_src, tile_used, tile_next, tile_slot, xs, w_exp_in, w_exp_out, layer, cfg)
    return _combine_call(dest3, y, gt, ys, modr, layer, ln_g, ln_b, cfg, split_out), xs


def _dft_tables(cfg):
    D, R, L = cfg.D, cfg.R, cfg.SEQ
    gd = D // cfg.F_GROUPS

    def cs(n, scale):
        k = np.arange(n)
        ang = 2.0 * np.pi * ((k[:, None] * k[None, :]) % n) / n
        return np.cos(ang) * scale, np.sin(ang) * scale

    cc, sc = cs(gd, 1.0 / math.sqrt(gd))
    chan = jnp.asarray(np.concatenate([cc, -sc], axis=1), bf16)
    cl, sl = cs(L, 1.0 / math.sqrt(L))
    cr, sr = cs(R, 1.0 / math.sqrt(R))
    J, KB = cfg.FFT_J, cfg.FFT_KB
    f2 = np.concatenate([cr, -sr], axis=0)
    k1m = jnp.asarray(np.kron(f2, np.eye(J)), bf16)
    eye = np.eye(KB)
    k2c = jnp.asarray(np.einsum("kn,ab->kabn", cr, eye).reshape(R * KB, KB * R), bf16)
    k2s = jnp.asarray(np.einsum("kn,ab->kabn", sr, eye).reshape(R * KB, KB * R), bf16)
    k1 = np.arange(R)
    ang = 2.0 * np.pi * (k1[:, None] * k1[None, :]) / (R * R)
    ang = ang.reshape(R, R // J, J).transpose(1, 0, 2).reshape(R // J, R * J)
    tw_re = jnp.asarray(np.repeat(np.cos(ang)[:, :, None], 128, axis=2), f32)
    tw_im = jnp.asarray(np.repeat(-np.sin(ang)[:, :, None], 128, axis=2), f32)
    return dict(chan=chan, c_seq=jnp.asarray(cl, bf16), s_seq=jnp.asarray(sl, bf16), k1m=k1m,
                k2c=k2c, k2s=k2s, tw_re=tw_re, tw_im=tw_im)


def _rot_cols(w):
    shp = w.shape
    w2 = w.reshape(shp[:-1] + (shp[-1] // 2, 2))
    return jnp.stack([-w2[..., 1], w2[..., 0]], axis=-1).reshape(shp)


def _rope_tables(cfg):
    L, RP = cfg.seg_len, cfg.ROPE
    pairs = RP // 4
    rows = L // cfg.GRID_W
    t_row = jnp.repeat(jnp.arange(rows, dtype=f32), cfg.GRID_W)
    t_col = jnp.tile(jnp.arange(cfg.GRID_W, dtype=f32), rows)
    inv = ROPE_BASE ** (-jnp.arange(pairs, dtype=f32) / pairs)
    ang = jnp.concatenate([t_row[:, None] * inv, t_col[:, None] * inv], -1)
    cos = jnp.repeat(jnp.cos(ang), 2, axis=-1)
    sin = jnp.repeat(jnp.sin(ang), 2, axis=-1)
    cos_t = jnp.concatenate([jnp.ones((L, RP), f32), jnp.tile(cos, (cfg.DEC_BATCH, 1))], 0)
    sin_t = jnp.concatenate([jnp.zeros((L, RP), f32), jnp.tile(sin, (cfg.DEC_BATCH, 1))], 0)
    return cos_t, sin_t


def _mla_weights(w_a, w_uq, w_ukv, cfg):
    D, H, QL, KVL, NP, RP, VD = cfg.D, cfg.H, cfg.QL, cfg.KVL, cfg.NOPE, cfg.ROPE, cfg.VD
    kr_w = w_a[:, QL + KVL:]
    z = jnp.zeros((D, 128 - RP), f32)
    w_a2 = jnp.concatenate([w_a[:, :QL + KVL], kr_w, z, _rot_cols(kr_w), z], axis=1).astype(bf16)
    wq = w_uq.reshape(QL, H, NP + RP)
    w_q = jnp.concatenate([wq, _rot_cols(wq[..., NP:])], axis=-1).transpose(1, 0, 2).astype(bf16)
    w_kv = w_ukv.reshape(KVL, H, NP + VD).transpose(1, 0, 2).astype(bf16)
    return w_a2, w_q, w_kv


def _forward(cfg, x_prompt, x_sample, c, cache_ckv, cache_krope, c_ctx, w_ada, b_ada, w_f_in, w_f_out,
             w_mla_a, g_mla_q, g_mla_kv, w_mla_uq, w_mla_ukv, w_mla_o, ln_mix_g, ln_mix_b,
             ln_ffn_g, ln_ffn_b, w_router, b_router, w_exp_in, w_exp_out):
    assert cfg.BATCH * cfg.SEQ == cfg.seg_len and cfg.R * cfg.R == cfg.seg_len
    assert cfg.TQ % cfg.SEQ == 0 and cfg.LK % cfg.TKV == 0
    assert cfg.R % cfg.FFT_J == 0 and cfg.R % cfg.FFT_KB == 0
    N, D, L, E = cfg.N, cfg.D, cfg.seg_len, cfg.E
    assert cfg.DEPTH >= 2
    y = (x_prompt.reshape(L, D), x_sample.reshape(cfg.DEC_BATCH * L, D))

    cvec = jnp.concatenate([c_ctx[None], c, jnp.zeros((MOD_ROWS - cfg.n_seg, D), f32)], axis=0)
    mod = _ada_call(cvec, w_ada, b_ada, cfg)
    modr = mod.reshape(cfg.DEPTH * MOD_ROWS * 6, 1, D)

    tabs = _dft_tables(cfg)
    cos_t, sin_t = _rope_tables(cfg)
    w_hi = w_router.astype(bf16)
    w_lo = (w_router - w_hi.astype(f32)).astype(bf16)
    wr2 = jnp.concatenate([jnp.concatenate([w_hi, w_lo], 1),
                           jnp.concatenate([w_hi, jnp.zeros_like(w_lo)], 1)], 0)
    rc = cfg.RC
    tri = jnp.asarray(np.triu(np.ones((rc, rc), np.float32)), bf16)

    ckv_out, kr_out = [], []
    xs_buf = jnp.zeros((2 * N + E * cfg.TE, D // 2), jnp.uint32)
    for l in range(cfg.DEPTH):
        j = l // 2
        if l % 2 == 0:
            yt = y if isinstance(y, tuple) else (y,)
            a = _fourier_in_call(yt, modr, l, w_f_in[j].astype(bf16), tabs["chan"], cfg)
            f_p = _prompt_dft_call(a, tabs["c_seq"], tabs["s_seq"], cfg)
            t = _fft_stage1_call(a, tabs["k1m"], tabs["tw_re"], tabs["tw_im"], cfg)
            f_s = _fft_stage2_call(t, tabs["k2c"], tabs["k2s"], cfg)
            y = _post_call((f_p, f_s), w_f_out[j].astype(bf16), yt, modr, l, 2, ln_mix_g[l], ln_mix_b[l], cfg)
        else:
            w_a2, w_q, w_kv = _mla_weights(w_mla_a[j], w_mla_uq[j], w_mla_ukv[j], cfg)
            cq, ckv, kr, krr = _mla_a_call(y, modr, l, w_a2, g_mla_q[j], g_mla_kv[j], cos_t, sin_t, cfg)
            ckv_out.append(ckv[:L].reshape(cfg.BATCH, cfg.SEQ, cfg.KVL))
            kr_out.append(kr[:L].reshape(cfg.BATCH, cfg.SEQ, cfg.ROPE))
            q = _q_call(cq, w_q, cos_t, sin_t, cfg)
            pad = cfg.PAST
            ckv_all = jnp.concatenate([
                jnp.concatenate([ckv[:L], jnp.zeros((pad, cfg.KVL), f32)], 0)[None],
                jnp.concatenate([cache_ckv[:, j], ckv[L:].reshape(cfg.DEC_BATCH, L, cfg.KVL)], 1)], 0)
            kr_all = jnp.concatenate([
                jnp.concatenate([krr[:L], jnp.zeros((pad, cfg.ROPE), f32)], 0)[None],
                jnp.concatenate([cache_krope[:, j], krr[L:].reshape(cfg.DEC_BATCH, L, cfg.ROPE)], 1)], 0)
            k, v = _kv_call(ckv_all, kr_all, w_kv, cfg)
            o = _attention_call(q, k, v, cfg)
            y = _post_call((o,), w_mla_o[j].astype(bf16), (y,), modr, l, 2, ln_mix_g[l], ln_mix_b[l], cfg)
        y, xs_buf = _moe_layer(y, xs_buf, modr, l, wr2, b_router, tri, w_exp_in, w_exp_out, ln_ffn_g[l],
                               ln_ffn_b[l], cfg, split_out=(l == cfg.DEPTH - 1))

    y_p = y[0].reshape(cfg.BATCH, cfg.SEQ, D)
    y_s = y[1].reshape(cfg.DEC_BATCH, L, D)
    return y_p, y_s, jnp.stack(ckv_out, axis=1), jnp.stack(kr_out, axis=1)


def kernel(x_prompt, x_sample, c, cache_ckv, cache_krope, c_ctx, w_ada, b_ada, w_f_in, w_f_out, w_mla_a, g_mla_q, g_mla_kv, w_mla_uq, w_mla_ukv, w_mla_o, ln_mix_g, ln_mix_b, ln_ffn_g, ln_ffn_b, w_router, b_router, w_exp_in, w_exp_out):
    return _forward(Cfg(), x_prompt, x_sample, c, cache_ckv, cache_krope, c_ctx, w_ada, b_ada, w_f_in,
                    w_f_out, w_mla_a, g_mla_q, g_mla_kv, w_mla_uq, w_mla_ukv, w_mla_o, ln_mix_g,
                    ln_mix_b, ln_ffn_g, ln_ffn_b, w_router, b_router, w_exp_in, w_exp_out)
```
